```python
import jax
import jax.numpy as jnp
from jax import lax
import numpy as np

D_MODEL = 1024
BATCH = 16
SEQ = 2048
DEPTH = 4

CTX_LEN = 256
GRID_W = 64
HEAD_DIM = 64
BRANCH_WIDTH = 512
N_BRANCH = 3
A_HEADS = 8
A_KV_HEADS = 2
WINDOW = 128
B_HEADS = 8
B_KV_HEADS = 2
C_HEADS = 4
C_DK = 128
C_DV = 128
CHUNK = 64
BLOCK = 128
D_FF = 2816
CONV_W = 3
ROPE_THETA = 10000.0
AXIS_ROPE_DIM = HEAD_DIM // 2
N_MOD = 6
EPS = 1e-6
NEG_BIG = -1e30
PROJ_SIZES = (A_HEADS * HEAD_DIM, A_KV_HEADS * HEAD_DIM, A_KV_HEADS * HEAD_DIM,
              B_HEADS * HEAD_DIM, B_KV_HEADS * HEAD_DIM, B_KV_HEADS * HEAD_DIM,
              C_HEADS * C_DK, C_HEADS * C_DK, C_HEADS * C_DK, C_HEADS * C_DV, C_HEADS * C_DV,
              N_BRANCH * D_MODEL)
PROJ_WIDTH = sum(PROJ_SIZES)

kernel_name = 'hybrid_gated_swa_axial_hgrn2_dit'


def rms_norm(t, g):
    tf = t.astype(jnp.float32)
    y = tf * lax.rsqrt(jnp.mean(tf * tf, axis=-1, keepdims=True) + EPS)
    return (y * g.astype(jnp.float32)).astype(t.dtype)


def axial_rope(n):
    rows = n // GRID_W
    row = jnp.repeat(jnp.arange(rows), GRID_W).astype(jnp.float32)
    col = jnp.tile(jnp.arange(GRID_W), rows).astype(jnp.float32)
    inv = ROPE_THETA ** (-jnp.arange(0, AXIS_ROPE_DIM, 2, dtype=jnp.float32) / AXIS_ROPE_DIM)
    ang = jnp.concatenate([row[:, None] * inv, col[:, None] * inv], axis=-1)
    return jnp.cos(ang), jnp.sin(ang)


def apply_rope(t, cos, sin):
    t1, t2 = t[..., 0::2], t[..., 1::2]
    c = cos[None, :, None, :].astype(t.dtype)
    s = sin[None, :, None, :].astype(t.dtype)
    return jnp.stack([t1 * c - t2 * s, t1 * s + t2 * c], axis=-1).reshape(t.shape)


def features(h, w_in, qn, kn, rope):
    B_, L, _ = h.shape
    split_at = np.cumsum(PROJ_SIZES)[:-1].tolist()
    aq, ak, av, bq, bk, bv, cq, cf, cb, ci, cg, gates = jnp.split(h @ w_in, split_at, axis=-1)
    heads = lambda t, n: t.reshape(B_, L, n, HEAD_DIM)
    aq, ak, av = heads(aq, A_HEADS), heads(ak, A_KV_HEADS), heads(av, A_KV_HEADS)
    bq = rms_norm(heads(bq, B_HEADS), qn)
    bk = rms_norm(heads(bk, B_KV_HEADS), kn)
    bv = heads(bv, B_KV_HEADS)
    if rope is not None:
        cos, sin = rope
        aq, ak, bq, bk = [apply_rope(t, cos, sin) for t in (aq, ak, bq, bk)]
    return aq, ak, av, bq, bk, bv, cq, cf, cb, ci, cg, gates


def dense_attn(q, k, v, sink=None):
    B_, Lq, HQ, hd = q.shape
    KV = k.shape[2]
    G = HQ // KV
    s = jnp.einsum('bqkgd,bjkd->bkgqj', q.reshape(B_, Lq, KV, G, hd), k).astype(jnp.float32) * hd ** -0.5
    if sink is not None:
        s_sink = jnp.broadcast_to(sink.astype(jnp.float32).reshape(1, KV, G, 1, 1), s.shape[:-1] + (1,))
        s = jnp.concatenate([s, s_sink], axis=-1)
    p = jax.nn.softmax(s, axis=-1)[..., :k.shape[1]].astype(v.dtype)
    return jnp.einsum('bkgqj,bjkd->bqkgd', p, v).reshape(B_, Lq, HQ * hd)


def window_attn(q, k, v, ck, cv, sink):
    B_, S, HQ, hd = q.shape
    KV = k.shape[2]
    G = HQ // KV
    nb = S // BLOCK
    n_ctx = ck.shape[1]
    qb = q.reshape(B_, nb, BLOCK, KV, G, hd)

    def band(t):
        tp = jnp.pad(t, ((0, 0), (BLOCK, BLOCK), (0, 0), (0, 0))).reshape(B_, nb + 2, BLOCK, KV, hd)
        return jnp.concatenate([tp[:, :-2], tp[:, 1:-1], tp[:, 2:]], axis=2)

    kb, vb = band(k), band(v)
    scale = hd ** -0.5
    s_loc = jnp.einsum('bnqkgd,bnjkd->bkgnqj', qb, kb).astype(jnp.float32) * scale
    blk = jnp.arange(nb)[:, None, None]
    qpos = blk * BLOCK + jnp.arange(BLOCK)[None, :, None]
    kpos = (blk - 1) * BLOCK + jnp.arange(3 * BLOCK)[None, None, :]
    valid = (jnp.abs(kpos - qpos) <= WINDOW) & (kpos >= 0) & (kpos < S)
    s_loc = jnp.where(valid, s_loc, NEG_BIG)
    s_ctx = jnp.einsum('bnqkgd,bjkd->bkgnqj', qb, ck).astype(jnp.float32) * scale
    s_sink = jnp.broadcast_to(sink.astype(jnp.float32).reshape(1, KV, G, 1, 1, 1), s_ctx.shape[:-1] + (1,))
    p = jax.nn.softmax(jnp.concatenate([s_ctx, s_loc, s_sink], axis=-1), axis=-1).astype(v.dtype)
    o = (jnp.einsum('bkgnqj,bjkd->bnqkgd', p[..., :n_ctx], cv)
         + jnp.einsum('bkgnqj,bnjkd->bnqkgd', p[..., n_ctx:n_ctx + 3 * BLOCK], vb))
    return o.reshape(B_, S, HQ * hd)


def global_attn(q, k, v, ck, cv):
    B_, S, HQ, hd = q.shape
    nb = S // BLOCK
    k_all = jnp.concatenate([ck, k], axis=1)
    v_all = jnp.concatenate([cv, v], axis=1)
    qb = jnp.moveaxis(q.reshape(B_, nb, BLOCK, HQ, hd), 1, 0)
    o = lax.map(lambda qblk: dense_attn(qblk, k_all, v_all), qb)
    return jnp.moveaxis(o, 0, 1).reshape(B_, S, HQ * hd)


def layer_lower_bounds(raw):
    p = jax.nn.softmax(raw.astype(jnp.float32), axis=0)
    return jnp.clip(jnp.cumsum(p, axis=0) - p[0], 0.0, 1.0)


def hgrn_forget(z, lb):
    log_f = jax.nn.log_sigmoid(z) + jnp.log1p(lb * jnp.exp(-z))
    k = (1.0 - lb) * jax.nn.sigmoid(-z)
    return k, log_f


def gla_chunk_scan(q, k, v, log_f, s0):
    B_, L, H, _ = q.shape
    nc = L // CHUNK
    to_chunks = lambda t: t.reshape(B_, nc, CHUNK, H, t.shape[-1]).transpose(1, 0, 3, 2, 4)
    tri = jnp.tril(jnp.ones((CHUNK, CHUNK), dtype=bool))[:, :, None]

    def step(s, inp):
        qi, ki, vi, fi = inp
        a = jnp.cumsum(fi, axis=2)
        o_inter = jnp.einsum('bhtd,bhde->bhte', qi * jnp.exp(a), s)
        diff = a[:, :, :, None, :] - a[:, :, None, :, :]
        decay = jnp.exp(jnp.where(tri, diff, NEG_BIG))
        scores = jnp.einsum('bhtd,bhjd,bhtjd->bhtj', qi, ki, decay)
        o_intra = jnp.einsum('bhtj,bhje->bhte', scores, vi)
        a_last = a[:, :, -1, :]
        s_new = (jnp.exp(a_last)[..., None] * s
                 + jnp.einsum('bhjd,bhje->bhde', ki * jnp.exp(a_last[:, :, None, :] - a), vi))
        return s_new, o_inter + o_intra

    s_fin, o = lax.scan(step, s0, (to_chunks(q), to_chunks(k), to_chunks(v), to_chunks(log_f)))
    return o.transpose(1, 0, 3, 2, 4).reshape(B_, L, H, v.shape[-1]), s_fin


def hgrn2_bidir(cq, cf, cb, ci, cg, lb_f, lb_b, gn, s0_f, s0_b):
    B_, L = cq.shape[:2]
    q = jax.nn.silu(cq.astype(jnp.float32)).reshape(B_, L, C_HEADS, C_DK)
    v = ci.astype(jnp.float32).reshape(B_, L, C_HEADS, C_DV)
    k_f, lf_f = hgrn_forget(cf.astype(jnp.float32).reshape(B_, L, C_HEADS, C_DK), lb_f.reshape(C_HEADS, C_DK))
    k_b, lf_b = hgrn_forget(cb.astype(jnp.float32).reshape(B_, L, C_HEADS, C_DK), lb_b.reshape(C_HEADS, C_DK))
    rev = lambda t: jnp.flip(t, axis=1)
    o_f, s_f = gla_chunk_scan(q, k_f, v, lf_f, s0_f)
    o_b, s_b = gla_chunk_scan(rev(q), rev(k_b), rev(v), rev(lf_b), s0_b)
    o = rms_norm(o_f + rev(o_b), gn) * jax.nn.silu(cg.astype(jnp.float32)).reshape(B_, L, C_HEADS, C_DV)
    return o.reshape(B_, L, C_HEADS * C_DV).astype(cq.dtype), s_f, s_b


def merge_branches(ya, yb, yc, gates, w_branch, w_out):
    B_, L = ya.shape[:2]
    ys = jnp.stack([ya, yb, yc], axis=2)
    up = jnp.einsum('blnw,nwd->blnd', ys, w_branch)
    g = jax.nn.sigmoid(gates.reshape(B_, L, N_BRANCH, -1))
    return jnp.sum(g * up, axis=2) @ w_out


def dwconv(u, w, b):
    y = lax.conv_general_dilated(u, w[:, None, :], window_strides=(1,),
                                 padding=((CONV_W // 2, CONV_W // 2),),
                                 dimension_numbers=('NWC', 'WIO', 'NWC'),
                                 feature_group_count=u.shape[-1])
    return y + b


def conv_ffn(h, w_up, w_dw, b_dw, w_down):
    a, g = jnp.split(dwconv(h @ w_up, w_dw, b_dw), 2, axis=-1)
    return (a * jax.nn.silu(g)) @ w_down


def setup_inputs(seed: int = 0) -> dict:
    key = jax.random.key(seed)
    ks = jax.random.split(key, 22)
    nrm = lambda k, shape, s: jax.random.normal(k, shape, jnp.float32) * s
    L = DEPTH
    return {
        'x': nrm(ks[0], (BATCH, SEQ, D_MODEL), 1.0),
        'c': nrm(ks[1], (BATCH, D_MODEL), 1.0),
        'ctx': nrm(ks[2], (BATCH, CTX_LEN, D_MODEL), 1.0),
        'c_ctx': nrm(ks[3], (D_MODEL,), 1.0),
        'w_mod': nrm(ks[4], (L, D_MODEL, N_MOD * D_MODEL), 0.5 * D_MODEL ** -0.5),
        'b_mod': nrm(ks[5], (L, N_MOD * D_MODEL), 0.02),
        'norm_mix': 1.0 + nrm(ks[6], (L, D_MODEL), 0.05),
        'norm_ffn': 1.0 + nrm(ks[7], (L, D_MODEL), 0.05),
        'w_in': nrm(ks[8], (L, D_MODEL, PROJ_WIDTH), D_MODEL ** -0.5),
        'sink_a': nrm(ks[9], (L, A_HEADS), 0.5),
        'qn_b': 1.0 + nrm(ks[10], (L, HEAD_DIM), 0.05),
        'kn_b': 1.0 + nrm(ks[11], (L, HEAD_DIM), 0.05),
        'lb_fwd': nrm(ks[12], (L, C_HEADS * C_DK), 1.0),
        'lb_bwd': nrm(ks[13], (L, C_HEADS * C_DK), 1.0),
        'gn_c': 1.0 + nrm(ks[14], (L, C_DV), 0.05),
        'w_branch': nrm(ks[15], (L, N_BRANCH, BRANCH_WIDTH, D_MODEL), BRANCH_WIDTH ** -0.5),
        'w_out': nrm(ks[16], (L, D_MODEL, D_MODEL), D_MODEL ** -0.5),
        'w_up': nrm(ks[17], (L, D_MODEL, 2 * D_FF), D_MODEL ** -0.5),
        'w_dw': nrm(ks[18], (L, CONV_W, 2 * D_FF), CONV_W ** -0.5),
        'b_dw': nrm(ks[19], (L, 2 * D_FF), 0.02),
        'w_down': nrm(ks[20], (L, D_FF, D_MODEL), D_FF ** -0.5),
        'norm_final': 1.0 + nrm(ks[21], (D_MODEL,), 0.05),
    }


def reference(x, c, ctx, c_ctx, w_mod, b_mod, norm_mix, norm_ffn, w_in, sink_a, qn_b, kn_b,
              lb_fwd, lb_bwd, gn_c, w_branch, w_out, w_up, w_dw, b_dw, w_down, norm_final):
    B_, S, D = x.shape
    rope = axial_rope(S)
    lbf = layer_lower_bounds(lb_fwd)
    lbb = layer_lower_bounds(lb_bwd)
    s0 = jnp.zeros((B_, C_HEADS, C_DK, C_DV), jnp.float32)
    for l in range(DEPTH):
        last = l == DEPTH - 1
        m_lat = (jax.nn.silu(c) @ w_mod[l] + b_mod[l]).reshape(B_, 1, N_MOD, D)
        m_ctx = (jax.nn.silu(c_ctx) @ w_mod[l] + b_mod[l]).reshape(N_MOD, D)

        h_lat = rms_norm(x, norm_mix[l]) * (1.0 + m_lat[:, :, 1]) + m_lat[:, :, 0]
        h_ctx = rms_norm(ctx, norm_mix[l]) * (1.0 + m_ctx[1]) + m_ctx[0]
        aq, ak, av, bq, bk, bv, cq, cf, cb, ci, cg, gates = features(h_lat, w_in[l], qn_b[l], kn_b[l], rope)
        caq, cak, cav, cbq, cbk, cbv, ccq, ccf, ccb, cci, ccg, cgates = features(h_ctx, w_in[l], qn_b[l], kn_b[l], None)

        yc_ctx, s_f, s_b = hgrn2_bidir(ccq, ccf, ccb, cci, ccg, lbf[l], lbb[l], gn_c[l], s0, s0)
        yc = hgrn2_bidir(cq, cf, cb, ci, cg, lbf[l], lbb[l], gn_c[l], s_f, s_b)[0]
        ya = window_attn(aq, ak, av, cak, cav, sink_a[l])
        yb = global_attn(bq, bk, bv, cbk, cbv)
        x_mix = x + m_lat[:, :, 2] * merge_branches(ya, yb, yc, gates, w_branch[l], w_out[l])

        h = rms_norm(x_mix, norm_ffn[l]) * (1.0 + m_lat[:, :, 4]) + m_lat[:, :, 3]
        x = x_mix + m_lat[:, :, 5] * conv_ffn(h, w_up[l], w_dw[l], b_dw[l], w_down[l])

        if not last:
            ya_c = dense_attn(caq, cak, cav, sink_a[l])
            yb_c = dense_attn(cbq, cbk, cbv)
            ctx_mix = ctx + m_ctx[2] * merge_branches(ya_c, yb_c, yc_ctx, cgates, w_branch[l], w_out[l])
            hc = rms_norm(ctx_mix, norm_ffn[l]) * (1.0 + m_ctx[4]) + m_ctx[3]
            ctx = ctx_mix + m_ctx[5] * conv_ffn(hc, w_up[l], w_dw[l], b_dw[l], w_down[l])
    return rms_norm(x, norm_final)
```

```python
import functools

import numpy as np
import jax
import jax.numpy as jnp
from jax import lax
from jax.experimental import pallas as pl
from jax.experimental.pallas import tpu as pltpu

HEAD_DIM = 64
A_HEADS = 8
A_KV_HEADS = 2
B_HEADS = 8
B_KV_HEADS = 2
C_HEADS = 4
C_DK = 128
C_DV = 128
N_BRANCH = 3
WINDOW = 128
BLOCK = 128
GRID_W = 64
ROPE_THETA = 10000.0
N_MOD = 6
EPS = 1e-6
NEG_BIG = -1e30
Q_WIDTH = A_HEADS * HEAD_DIM
KV_WIDTH = A_KV_HEADS * HEAD_DIM
C_WIDTH = C_HEADS * C_DK
FEAT_WIDTH = 2 * Q_WIDTH + 4 * KV_WIDTH + 5 * C_WIDTH
LANES = 128
SUBLANES = 8
VMEM_LIMIT = 56 * 1024 * 1024

F32 = jnp.float32
BF16 = jnp.bfloat16


def _cparams(*sem):
    return pltpu.CompilerParams(dimension_semantics=sem, vmem_limit_bytes=VMEM_LIMIT)


def _resident(shape):
    zeros = (0,) * len(shape)
    return pl.BlockSpec(shape, lambda *_: zeros, pipeline_mode=pl.Buffered(1))


def _dot(a, b):
    return jnp.dot(a, b, preferred_element_type=F32)


def _dot_nt(a, b):
    return lax.dot_general(a, b, (((1,), (1,)), ((), ())), preferred_element_type=F32)


def _dot_tn(a, b):
    return lax.dot_general(a, b, (((0,), (0,)), ((), ())), preferred_element_type=F32)


def _split3(x):
    hi = x.astype(BF16)
    r1 = x - hi.astype(F32)
    mid = r1.astype(BF16)
    lo = (r1 - mid.astype(F32)).astype(BF16)
    return hi, mid, lo


def _dot_exact_rhs(m_bf16, x):
    hi, mid, lo = _split3(x)
    return _dot(m_bf16, hi) + _dot(m_bf16, mid) + _dot(m_bf16, lo)


def _dot_exact_lhs(x, m_bf16):
    hi, mid, lo = _split3(x)
    return _dot(hi, m_bf16) + _dot(mid, m_bf16) + _dot(lo, m_bf16)


def _silu(x):
    return x * (1.0 / (1.0 + jnp.exp(-x)))


def _sigmoid(x):
    return 1.0 / (1.0 + jnp.exp(-x))


def _adaln(x, g, shift, scale):
    y = x * lax.rsqrt(jnp.mean(x * x, axis=-1, keepdims=True) + EPS)
    return (y * g) * (1.0 + scale) + shift


def _mod_rows(mlat_ref, mctx_ref, is_ctx):
    return jnp.where(is_ctx, mctx_ref[...], mlat_ref[0])


def _mod_kernel(c_ref, w_ref, b_ref, o_ref):
    sc = _silu(c_ref[...]).astype(BF16)
    o_ref[0] = _dot(sc, w_ref[0].astype(BF16)) + b_ref[0]


def _modulation(c_rows, w_mod, b_mod):
    depth, d, n = w_mod.shape
    rows = c_rows.shape[0]
    tn = d
    return pl.pallas_call(
        _mod_kernel,
        out_shape=jax.ShapeDtypeStruct((depth, rows, n), F32),
        grid=(depth, n // tn),
        in_specs=[pl.BlockSpec((rows, d), lambda l, j: (0, 0)),
                  pl.BlockSpec((1, d, tn), lambda l, j: (l, 0, j)),
                  pl.BlockSpec((1, 1, tn), lambda l, j: (l, 0, j))],
        out_specs=pl.BlockSpec((1, rows, tn), lambda l, j: (l, 0, j)),
        compiler_params=_cparams("parallel", "parallel"),
        name="modulation",
    )(c_rows, w_mod, b_mod.reshape(depth, 1, n))


def _rope_group(t, cos, sin_signed, first_half):
    partner = jnp.where(first_half, pltpu.roll(t, LANES - HEAD_DIM // 2, axis=1),
                        pltpu.roll(t, HEAD_DIM // 2, axis=1))
    return t * cos + partner * sin_signed


def _head_rms(t, gain, bd):
    ss = _dot_exact_lhs(t * t, bd)
    return t * lax.rsqrt(ss * (1.0 / HEAD_DIM) + EPS) * gain


def _forget_features(z, lb):
    e = jnp.exp(-jnp.abs(z))
    log_sig = jnp.minimum(z, 0.0) - jnp.log1p(e)
    log_f = log_sig + jnp.log1p(lb * jnp.exp(-z))
    inv = 1.0 / (1.0 + e)
    sig_neg = jnp.where(z >= 0.0, e * inv, inv)
    return (1.0 - lb) * sig_neg, log_f


def _proj_kernel(x_ref, mlat_ref, mctx_ref, g_ref, w_ref, cos_ref, sin_ref, qn_ref, kn_ref,
                 lbf_ref, lbb_ref, bd_ref,
                 qa_ref, qb_ref, kv_ref, hq_ref, hk_ref, hlf_ref, hv_ref, hog_ref, gt_ref,
                 *, ctx_len, tm, d_model):
    is_ctx = pl.program_id(1) * tm < ctx_len
    mod = _mod_rows(mlat_ref, mctx_ref, is_ctx)
    h = _adaln(x_ref[0], g_ref[...], mod[0:1], mod[1:2]).astype(BF16)
    cos = cos_ref[...]
    sin = sin_ref[...]
    bd = bd_ref[...]
    lane = lax.broadcasted_iota(jnp.int32, (tm, LANES), 1)
    first_half = (lane & (HEAD_DIM - 1)) < HEAD_DIM // 2
    scale = HEAD_DIM ** -0.5
    groups = Q_WIDTH // LANES

    def col(i):
        return _dot(h, w_ref[:, i * Q_WIDTH:(i + 1) * Q_WIDTH])

    def lanes(t, gidx):
        return t[:, gidx * LANES:(gidx + 1) * LANES]

    aq = col(0)
    qa_ref[0] = jnp.concatenate(
        [_rope_group(lanes(aq, gi), cos, sin, first_half) * scale for gi in range(groups)],
        axis=1).astype(BF16)
    bq = col(1)
    qb_ref[0] = jnp.concatenate(
        [_rope_group(_head_rms(lanes(bq, gi), qn_ref[...], bd), cos, sin, first_half) * scale
         for gi in range(groups)], axis=1).astype(BF16)
    kvp = col(2)
    kv_ref[0] = jnp.concatenate(
        [_rope_group(lanes(kvp, 0), cos, sin, first_half),
         _rope_group(_head_rms(lanes(kvp, 1), kn_ref[...], bd), cos, sin, first_half),
         lanes(kvp, 2), lanes(kvp, 3)], axis=1).astype(BF16)
    hq_ref[0] = _silu(col(3))
    kf, lff = _forget_features(col(4), lbf_ref[...])
    hk_ref[0, :, 0:C_WIDTH] = kf
    hlf_ref[0, :, 0:C_WIDTH] = lff
    kb, lfb = _forget_features(col(5), lbb_ref[...])
    hk_ref[0, :, C_WIDTH:2 * C_WIDTH] = kb
    hlf_ref[0, :, C_WIDTH:2 * C_WIDTH] = lfb
    hv_ref[0] = col(6)
    hog_ref[0] = _silu(col(7))
    for n in range(N_BRANCH):
        lo = FEAT_WIDTH + n * d_model
        gt_ref[0, :, n * d_model:(n + 1) * d_model] = _sigmoid(
            _dot(h, w_ref[:, lo:lo + d_model])).astype(BF16)


def _proj(x_all, m_lat, m_ctx, g, w, cos, sin, qn, kn, lbf, lbb, bd, *, ctx_len, tm):
    bsz, t, d = x_all.shape
    p = w.shape[1]
    row = lambda width: pl.BlockSpec((1, tm, width), lambda b, j: (b, j, 0))
    vec = lambda width: pl.BlockSpec((1, width), lambda b, j: (0, 0))
    outs = [(Q_WIDTH, BF16), (Q_WIDTH, BF16), (4 * KV_WIDTH, BF16), (C_WIDTH, F32),
            (2 * C_WIDTH, F32), (2 * C_WIDTH, F32), (C_WIDTH, F32), (C_WIDTH, F32),
            (N_BRANCH * d, BF16)]
    return pl.pallas_call(
        functools.partial(_proj_kernel, ctx_len=ctx_len, tm=tm, d_model=d),
        out_shape=[jax.ShapeDtypeStruct((bsz, t, wd), dt) for wd, dt in outs],
        grid=(bsz, t // tm),
        in_specs=[row(d),
                  pl.BlockSpec((1, N_MOD, d), lambda b, j: (b, 0, 0)),
                  pl.BlockSpec((N_MOD, d), lambda b, j: (0, 0)),
                  vec(d),
                  _resident((d, p)),
                  pl.BlockSpec((tm, LANES), lambda b, j: (j, 0)),
                  pl.BlockSpec((tm, LANES), lambda b, j: (j, 0)),
                  vec(LANES), vec(LANES), vec(C_WIDTH), vec(C_WIDTH),
                  pl.BlockSpec((LANES, LANES), lambda b, j: (0, 0))],
        out_specs=[row(wd) for wd, _ in outs],
        compiler_params=_cparams("parallel", "parallel"),
        name="proj",
    )(x_all, m_lat, m_ctx, g, w, cos, sin, qn, kn, lbf, lbb, bd)


def _scan_constants(chunk):
    idx = np.arange(chunk)
    mats = []
    for direction in (0, 1):
        tri = (idx[None, :] <= idx[:, None]) if direction == 0 else (idx[None, :] >= idx[:, None])
        tri = tri.astype(np.float32)
        blocks = [tri, 1.0 - tri]
        h = chunk // 2
        while h >= 1:
            start = (idx // (2 * h)) * (2 * h)
            ref = start + h - 1 if direction == 0 else start + h
            is_q = ((idx // h) % 2) != direction
            diff = tri - tri[ref]
            blocks.append(np.where(is_q[:, None], diff, -diff))
            h //= 2
        mats.append(np.concatenate(blocks, axis=0))
    return np.stack(mats)


def _scan_kernel(dmat_ref, q_ref, k_ref, lf_ref, v_ref, o_ref, st_ref, *, chunk):
    direction = pl.program_id(1)

    @pl.when(pl.program_id(2) == 0)
    def _():
        st_ref[...] = jnp.zeros_like(st_ref)

    n_lev = chunk.bit_length() - 1
    args = _dot_exact_rhs(dmat_ref[0], lf_ref[0])
    e_all = jnp.exp(args)
    e_a = e_all[0:chunk]
    e_b = e_all[chunk:2 * chunk]
    e_last = e_a[0:1] * e_b[0:1]

    row = lax.broadcasted_iota(jnp.int32, (chunk, 1), 0)
    r2 = lax.broadcasted_iota(jnp.int32, (chunk, chunk), 0)
    c2 = lax.broadcasted_iota(jnp.int32, (chunk, chunk), 1)
    q_all = q_ref[0]
    k_all = k_ref[0]
    v_all = v_ref[0]
    outs = []
    for hh in range(C_HEADS):
        sl = slice(hh * C_DK, (hh + 1) * C_DK)
        q, k, v = q_all[:, sl], k_all[:, sl], v_all[:, sl].astype(BF16)
        st = st_ref[hh]
        o = _dot_nt((q * e_a[:, sl]).astype(BF16), st.astype(BF16))
        scores = jnp.where(r2 == c2, _dot_nt(q.astype(BF16), k.astype(BF16)), 0.0)
        for lev in range(n_lev):
            shift = n_lev - 1 - lev
            is_q = ((row >> shift) & 1) != direction
            m = jnp.where(is_q, q, k) * e_all[(2 + lev) * chunk:(3 + lev) * chunk, sl]
            mq = jnp.where(is_q, m, 0.0).astype(BF16)
            mk = jnp.where(is_q, 0.0, m).astype(BF16)
            same = (r2 >> (shift + 1)) == (c2 >> (shift + 1))
            scores = scores + jnp.where(same, _dot_nt(mq, mk), 0.0)
        outs.append(o + _dot(scores.astype(BF16), v))
        kd = (k * e_b[:, sl]).astype(BF16)
        st_ref[hh] = st * e_last[:, sl] + _dot_tn(v, kd)
    o_ref[0, 0] = jnp.concatenate(outs, axis=1)


def _scan(dmats, hq, hk, hlf, hv, *, ctx_len, chunk):
    bsz, t, _ = hq.shape
    nc = t // chunk
    nctx = ctx_len // chunk

    def cidx(d, c):
        back = jnp.where(c < nctx, nctx - 1 - c, nc - 1 + nctx - c)
        return jnp.where(d == 0, c, back)

    rows = dmats.shape[1]
    return pl.pallas_call(
        functools.partial(_scan_kernel, chunk=chunk),
        out_shape=jax.ShapeDtypeStruct((2, bsz, t, C_WIDTH), F32),
        grid=(bsz, 2, nc),
        in_specs=[pl.BlockSpec((1, rows, chunk), lambda b, d, c: (d, 0, 0)),
                  pl.BlockSpec((1, chunk, C_WIDTH), lambda b, d, c: (b, cidx(d, c), 0)),
                  pl.BlockSpec((1, chunk, C_WIDTH), lambda b, d, c: (b, cidx(d, c), d)),
                  pl.BlockSpec((1, chunk, C_WIDTH), lambda b, d, c: (b, cidx(d, c), d)),
                  pl.BlockSpec((1, chunk, C_WIDTH), lambda b, d, c: (b, cidx(d, c), 0))],
        out_specs=pl.BlockSpec((1, 1, chunk, C_WIDTH), lambda b, d, c: (d, b, cidx(d, c), 0)),
        scratch_shapes=[pltpu.VMEM((C_HEADS, C_DV, C_DK), F32)],
        compiler_params=_cparams("parallel", "parallel", "arbitrary"),
        name="hgrn_scan",
    )(dmats, hq, hk, hlf, hv)


def _stack_heads(q, kvh, group):
    return jnp.concatenate(
        [q[:, (kvh * group + g) * HEAD_DIM:(kvh * group + g + 1) * HEAD_DIM] for g in range(group)],
        axis=0)


def _attn_a_kernel(sink_ref, q_ref, kv_ref, o_ref, *, ctx_len, t_len):
    i = pl.program_id(1)
    n_ctx_blocks = ctx_len // BLOCK
    group = A_HEADS // A_KV_HEADS
    band = 3 * BLOCK
    q = q_ref[0]

    def sink_col(kvh):
        r = lax.broadcasted_iota(jnp.int32, (group * BLOCK, 1), 0)
        col = jnp.full((group * BLOCK, 1), sink_ref[kvh * group], F32)
        for g in range(1, group):
            col = jnp.where(r >= g * BLOCK, sink_ref[kvh * group + g], col)
        return col

    def ctx_kv(kvh):
        kc = kv_ref[0, 0:ctx_len, kvh * HEAD_DIM:(kvh + 1) * HEAD_DIM]
        vc = kv_ref[0, 0:ctx_len, 2 * KV_WIDTH + kvh * HEAD_DIM:2 * KV_WIDTH + (kvh + 1) * HEAD_DIM]
        return kc, vc

    def finish(parts):
        o_ref[0] = jnp.concatenate(
            [o[g * BLOCK:(g + 1) * BLOCK] for o in parts for g in range(group)], axis=1).astype(BF16)

    @pl.when(i < n_ctx_blocks)
    def _():
        parts = []
        for kvh in range(A_KV_HEADS):
            q4 = _stack_heads(q, kvh, group)
            kc, vc = ctx_kv(kvh)
            sk = sink_col(kvh)
            s = _dot_nt(q4, kc)
            m = jnp.maximum(jnp.max(s, axis=-1, keepdims=True), sk)
            p = jnp.exp(s - m)
            den = jnp.sum(p, axis=-1, keepdims=True) + jnp.exp(sk - m)
            parts.append(_dot(p.astype(BF16), vc) * (1.0 / den))
        finish(parts)

    @pl.when(i >= n_ctx_blocks)
    def _():
        n = i - n_ctx_blocks
        start = pl.multiple_of(jnp.minimum(ctx_len + (n - 1) * BLOCK, t_len - band), BLOCK)
        rq = lax.broadcasted_iota(jnp.int32, (group * BLOCK, band), 0)
        ck = lax.broadcasted_iota(jnp.int32, (group * BLOCK, band), 1)
        qpos = n * BLOCK + (rq & (BLOCK - 1))
        kpos = start - ctx_len + ck
        valid = (jnp.abs(kpos - qpos) <= WINDOW) & (kpos >= 0)
        parts = []
        for kvh in range(A_KV_HEADS):
            q4 = _stack_heads(q, kvh, group)
            kc, vc = ctx_kv(kvh)
            kl = kv_ref[0, pl.ds(start, band), kvh * HEAD_DIM:(kvh + 1) * HEAD_DIM]
            vl = kv_ref[0, pl.ds(start, band),
                        2 * KV_WIDTH + kvh * HEAD_DIM:2 * KV_WIDTH + (kvh + 1) * HEAD_DIM]
            sk = sink_col(kvh)
            s_ctx = _dot_nt(q4, kc)
            s_loc = jnp.where(valid, _dot_nt(q4, kl), NEG_BIG)
            m = jnp.maximum(jnp.maximum(jnp.max(s_ctx, axis=-1, keepdims=True),
                                        jnp.max(s_loc, axis=-1, keepdims=True)), sk)
            p_ctx = jnp.exp(s_ctx - m)
            p_loc = jnp.exp(s_loc - m)
            den = (jnp.sum(p_ctx, axis=-1, keepdims=True) + jnp.sum(p_loc, axis=-1, keepdims=True)
                   + jnp.exp(sk - m))
            o = _dot(p_ctx.astype(BF16), vc) + _dot(p_loc.astype(BF16), vl)
            parts.append(o * (1.0 / den))
        finish(parts)


def _attn_a(sink, qa, kv, *, ctx_len):
    bsz, t, _ = qa.shape
    return pl.pallas_call(
        functools.partial(_attn_a_kernel, ctx_len=ctx_len, t_len=t),
        out_shape=jax.ShapeDtypeStruct((bsz, t, Q_WIDTH), BF16),
        grid=(bsz, t // BLOCK),
        in_specs=[pl.BlockSpec(memory_space=pltpu.SMEM),
                  pl.BlockSpec((1, BLOCK, Q_WIDTH), lambda b, i: (b, i, 0)),
                  pl.BlockSpec((1, t, 4 * KV_WIDTH), lambda b, i: (b, 0, 0))],
        out_specs=pl.BlockSpec((1, BLOCK, Q_WIDTH), lambda b, i: (b, i, 0)),
        compiler_params=_cparams("parallel", "arbitrary"),
        name="attn_window",
    )(sink, qa, kv)


def _attn_b_kernel(q_ref, kv_ref, o_ref, *, ctx_len, t_len):
    i = pl.program_id(1)
    n_ctx_blocks = ctx_len // BLOCK
    group = B_HEADS // B_KV_HEADS
    q = q_ref[0]

    def run(n_keys):
        parts = []
        for kvh in range(B_KV_HEADS):
            q4 = _stack_heads(q, kvh, group)
            k = kv_ref[0, 0:n_keys, KV_WIDTH + kvh * HEAD_DIM:KV_WIDTH + (kvh + 1) * HEAD_DIM]
            v = kv_ref[0, 0:n_keys, 3 * KV_WIDTH + kvh * HEAD_DIM:3 * KV_WIDTH + (kvh + 1) * HEAD_DIM]
            s = _dot_nt(q4, k)
            m = jnp.max(s, axis=-1, keepdims=True)
            p = jnp.exp(s - m)
            den = jnp.sum(p, axis=-1, keepdims=True)
            parts.append(_dot(p.astype(BF16), v) * (1.0 / den))
        o_ref[0] = jnp.concatenate(
            [o[g * BLOCK:(g + 1) * BLOCK] for o in parts for g in range(group)], axis=1).astype(BF16)

    @pl.when(i < n_ctx_blocks)
    def _():
        run(ctx_len)

    @pl.when(i >= n_ctx_blocks)
    def _():
        run(t_len)


def _attn_b(qb, kv, *, ctx_len):
    bsz, t, _ = qb.shape
    return pl.pallas_call(
        functools.partial(_attn_b_kernel, ctx_len=ctx_len, t_len=t),
        out_shape=jax.ShapeDtypeStruct((bsz, t, Q_WIDTH), BF16),
        grid=(bsz, t // BLOCK),
        in_specs=[pl.BlockSpec((1, BLOCK, Q_WIDTH), lambda b, i: (b, i, 0)),
                  pl.BlockSpec((1, t, 4 * KV_WIDTH), lambda b, i: (b, 0, 0))],
        out_specs=pl.BlockSpec((1, BLOCK, Q_WIDTH), lambda b, i: (b, i, 0)),
        compiler_params=_cparams("parallel", "arbitrary"),
        name="attn_global",
    )(qb, kv)


def _merge_kernel(x_ref, ya_ref, yb_ref, of_ref, ob_ref, og_ref, gt_ref, gn_ref, wb_ref, wo_ref,
                  mlat_ref, mctx_ref, o_ref, *, ctx_len, tm, d_model):
    is_ctx = pl.program_id(1) * tm < ctx_len
    mod = _mod_rows(mlat_ref, mctx_ref, is_ctx)
    oc = of_ref[0, 0] + ob_ref[0, 0]
    og = og_ref[0]
    gn = gn_ref[...]
    heads = []
    for hh in range(C_HEADS):
        sl = slice(hh * C_DV, (hh + 1) * C_DV)
        t = oc[:, sl]
        y = t * lax.rsqrt(jnp.mean(t * t, axis=-1, keepdims=True) + EPS) * gn
        heads.append(y * og[:, sl])
    yc = jnp.concatenate(heads, axis=1).astype(BF16)
    merged = None
    for n, y in enumerate((ya_ref[0], yb_ref[0], yc)):
        up = _dot(y, wb_ref[n])
        term = gt_ref[0, :, n * d_model:(n + 1) * d_model].astype(F32) * up
        merged = term if merged is None else merged + term
    out = _dot(merged.astype(BF16), wo_ref[...])
    o_ref[0] = x_ref[0] + mod[2:3] * out


def _merge(x_all, ya, yb, o_scan, hog, gates, gn, wb, wo, m_lat, m_ctx, *, ctx_len, tm):
    bsz, t, d = x_all.shape
    row = lambda width: pl.BlockSpec((1, tm, width), lambda b, j: (b, j, 0))
    return pl.pallas_call(
        functools.partial(_merge_kernel, ctx_len=ctx_len, tm=tm, d_model=d),
        out_shape=jax.ShapeDtypeStruct((bsz, t, d), F32),
        grid=(bsz, t // tm),
        in_specs=[row(d), row(Q_WIDTH), row(Q_WIDTH),
                  pl.BlockSpec((1, 1, tm, C_WIDTH), lambda b, j: (0, b, j, 0)),
                  pl.BlockSpec((1, 1, tm, C_WIDTH), lambda b, j: (1, b, j, 0)),
                  row(C_WIDTH), row(N_BRANCH * d),
                  pl.BlockSpec((1, C_DV), lambda b, j: (0, 0)),
                  _resident(wb.shape), _resident(wo.shape),
                  pl.BlockSpec((1, N_MOD, d), lambda b, j: (b, 0, 0)),
                  pl.BlockSpec((N_MOD, d), lambda b, j: (0, 0))],
        out_specs=row(d),
        compiler_params=_cparams("parallel", "parallel"),
        name="merge",
    )(x_all, ya, yb, o_scan, o_scan, hog, gates, gn, wb, wo, m_lat, m_ctx)


def _ffn_kernel(x_ref, xp_ref, xn_ref, mlat_ref, mctx_ref, g_ref, wu_ref, wdw_ref, bdw_ref, wd_ref,
                o_ref, *, ctx_len, t_len, tm, d_ff, fc):
    j = pl.program_id(1)
    is_ctx = j * tm < ctx_len
    mod = _mod_rows(mlat_ref, mctx_ref, is_ctx)
    x = x_ref[0]
    xe = jnp.concatenate([xp_ref[0], x, xn_ref[0]], axis=0)
    h = _adaln(xe, g_ref[...], mod[3:4], mod[4:5]).astype(BF16)
    rows = tm + 2 * SUBLANES
    r = j * tm + lax.broadcasted_iota(jnp.int32, (tm, 1), 0)
    first = (r == 0) | (r == ctx_len)
    last = (r == ctx_len - 1) | (r == t_len - 1)

    def conv(lo):
        u = _dot(h, wu_ref[:, lo:lo + fc])
        prev = jnp.where(first, 0.0, pltpu.roll(u, 1, axis=0)[SUBLANES:SUBLANES + tm])
        nxt = jnp.where(last, 0.0, pltpu.roll(u, rows - 1, axis=0)[SUBLANES:SUBLANES + tm])
        w = wdw_ref[:, lo:lo + fc]
        return (w[0:1] * prev + w[1:2] * u[SUBLANES:SUBLANES + tm] + w[2:3] * nxt
                + bdw_ref[:, lo:lo + fc])

    acc = None
    for ci in range(d_ff // fc):
        a = conv(ci * fc)
        gate = conv(d_ff + ci * fc)
        act = (a * _silu(gate)).astype(BF16)
        part = _dot(act, wd_ref[ci * fc:(ci + 1) * fc, :])
        acc = part if acc is None else acc + part
    o_ref[0] = x + mod[5:6] * acc


def _ffn(x_mix, m_lat, m_ctx, g, wu, wdw, bdw, wd, *, ctx_len, tm, fc):
    bsz, t, d = x_mix.shape
    d_ff = wd.shape[0]
    per = tm // SUBLANES
    last_blk = t // SUBLANES - 1
    return pl.pallas_call(
        functools.partial(_ffn_kernel, ctx_len=ctx_len, t_len=t, tm=tm, d_ff=d_ff, fc=fc),
        out_shape=jax.ShapeDtypeStruct((bsz, t, d), F32),
        grid=(bsz, t // tm),
        in_specs=[pl.BlockSpec((1, tm, d), lambda b, j: (b, j, 0)),
                  pl.BlockSpec((1, SUBLANES, d), lambda b, j: (b, jnp.maximum(j * per - 1, 0), 0)),
                  pl.BlockSpec((1, SUBLANES, d),
                               lambda b, j: (b, jnp.minimum((j + 1) * per, last_blk), 0)),
                  pl.BlockSpec((1, N_MOD, d), lambda b, j: (b, 0, 0)),
                  pl.BlockSpec((N_MOD, d), lambda b, j: (0, 0)),
                  pl.BlockSpec((1, d), lambda b, j: (0, 0)),
                  _resident(wu.shape),
                  pl.BlockSpec(wdw.shape, lambda b, j: (0, 0)),
                  pl.BlockSpec(bdw.shape, lambda b, j: (0, 0)),
                  _resident(wd.shape)],
        out_specs=pl.BlockSpec((1, tm, d), lambda b, j: (b, j, 0)),
        compiler_params=_cparams("parallel", "parallel"),
        name="conv_ffn",
    )(x_mix, x_mix, x_mix, m_lat, m_ctx, g, wu, wdw, bdw, wd)


def _final_kernel(x_ref, g_ref, o_ref):
    x = x_ref[0]
    o_ref[0] = x * lax.rsqrt(jnp.mean(x * x, axis=-1, keepdims=True) + EPS) * g_ref[...]


def _final_norm(x_all, g, *, ctx_len, tm):
    bsz, t, d = x_all.shape
    off = ctx_len // tm
    return pl.pallas_call(
        _final_kernel,
        out_shape=jax.ShapeDtypeStruct((bsz, t - ctx_len, d), F32),
        grid=(bsz, (t - ctx_len) // tm),
        in_specs=[pl.BlockSpec((1, tm, d), lambda b, j: (b, j + off, 0)),
                  pl.BlockSpec((1, d), lambda b, j: (0, 0))],
        out_specs=pl.BlockSpec((1, tm, d), lambda b, j: (b, j, 0)),
        compiler_params=_cparams("parallel", "parallel"),
        name="final_norm",
    )(x_all, g)


def _half_split(n_heads):
    base = np.concatenate([np.arange(0, HEAD_DIM, 2), np.arange(1, HEAD_DIM, 2)])
    return np.concatenate([h * HEAD_DIM + base for h in range(n_heads)])


def _proj_column_order(d_model):
    sizes = (Q_WIDTH, KV_WIDTH, KV_WIDTH, Q_WIDTH, KV_WIDTH, KV_WIDTH) + (C_WIDTH,) * 5 \
        + (N_BRANCH * d_model,)
    off = np.concatenate([[0], np.cumsum(sizes)])
    aq, ak, av, bq, bk, bv = (int(o) for o in off[:6])
    plain = lambda o, n: o + np.arange(n)
    return np.concatenate([
        aq + _half_split(A_HEADS), bq + _half_split(B_HEADS),
        ak + _half_split(A_KV_HEADS), bk + _half_split(B_KV_HEADS),
        plain(av, KV_WIDTH), plain(bv, KV_WIDTH),
        plain(int(off[6]), int(off[-1] - off[6]))])


def _rope_tables(ctx_len, seq):
    rows = seq // GRID_W
    row = jnp.repeat(jnp.arange(rows), GRID_W).astype(F32)
    col = jnp.tile(jnp.arange(GRID_W), rows).astype(F32)
    half = HEAD_DIM // 2
    inv = ROPE_THETA ** (-jnp.arange(0, half, 2, dtype=F32) / half)
    ang = jnp.concatenate([row[:, None] * inv, col[:, None] * inv], axis=-1)
    cos = jnp.concatenate([jnp.ones((ctx_len, half), F32), jnp.cos(ang)], axis=0)
    sin = jnp.concatenate([jnp.zeros((ctx_len, half), F32), jnp.sin(ang)], axis=0)
    reps = LANES // HEAD_DIM
    return (jnp.tile(jnp.concatenate([cos, cos], axis=1), (1, reps)),
            jnp.tile(jnp.concatenate([-sin, sin], axis=1), (1, reps)))


def _lower_bounds(raw):
    p = jax.nn.softmax(raw.astype(F32), axis=0)
    return jnp.clip(jnp.cumsum(p, axis=0) - p[0], 0.0, 1.0)


def _row_tile(ctx_len):
    tm = 256
    while ctx_len % tm:
        tm //= 2
    return tm


def _ffn_chunk(d_ff):
    for fc in (1408, 1024, 512, 256, 128):
        if d_ff % fc == 0:
            return fc
    return d_ff


def kernel(x, c, ctx, c_ctx, w_mod, b_mod, norm_mix, norm_ffn, w_in, sink_a, qn_b, kn_b, lb_fwd,
           lb_bwd, gn_c, w_branch, w_out, w_up, w_dw, b_dw, w_down, norm_final):
    bsz, seq, d = x.shape
    ctx_len = ctx.shape[1]
    depth = w_mod.shape[0]
    assert ctx_len % BLOCK == 0 and seq % BLOCK == 0 and ctx_len >= BLOCK and seq >= 3 * BLOCK
    assert d % LANES == 0
    tm = _row_tile(ctx_len)
    chunk = 64
    fc = _ffn_chunk(w_down.shape[1])

    n_rows = -(-(bsz + 1) // SUBLANES) * SUBLANES
    c_rows = jnp.concatenate([c, c_ctx[None], jnp.zeros((n_rows - bsz - 1, d), F32)], axis=0)
    mods = _modulation(c_rows, w_mod, b_mod)
    m_lat = mods[:, :bsz].reshape(depth, bsz, N_MOD, d)
    m_ctx = mods[:, bsz].reshape(depth, N_MOD, d)

    order = _proj_column_order(d)
    w_in_p = jnp.take(w_in, order, axis=2).astype(BF16)
    wb_bf = w_branch.astype(BF16)
    wo_bf = w_out.astype(BF16)
    wu_bf = w_up.astype(BF16)
    wd_bf = w_down.astype(BF16)
    head_order = _half_split(1)
    qn_p = jnp.tile(qn_b[:, head_order], (1, LANES // HEAD_DIM))
    kn_p = jnp.tile(kn_b[:, head_order], (1, LANES // HEAD_DIM))
    lbf = _lower_bounds(lb_fwd)
    lbb = _lower_bounds(lb_bwd)
    cos, sin = _rope_tables(ctx_len, seq)
    lane = np.arange(LANES)
    bd = jnp.asarray(lane[:, None] // HEAD_DIM == lane[None, :] // HEAD_DIM, BF16)
    dmats = jnp.asarray(_scan_constants(chunk), BF16)

    x_all = jnp.concatenate([ctx, x], axis=1)
    for l in range(depth):
        qa, qb, kv, hq, hk, hlf, hv, hog, gates = _proj(
            x_all, m_lat[l], m_ctx[l], norm_mix[l][None], w_in_p[l], cos, sin,
            qn_p[l][None], kn_p[l][None], lbf[l][None], lbb[l][None], bd, ctx_len=ctx_len, tm=tm)
        o_scan = _scan(dmats, hq, hk, hlf, hv, ctx_len=ctx_len, chunk=chunk)
        ya = _attn_a(sink_a[l], qa, kv, ctx_len=ctx_len)
        yb = _attn_b(qb, kv, ctx_len=ctx_len)
        x_mix = _merge(x_all, ya, yb, o_scan, hog, gates, gn_c[l][None], wb_bf[l], wo_bf[l],
                       m_lat[l], m_ctx[l], ctx_len=ctx_len, tm=tm)
        x_all = _ffn(x_mix, m_lat[l], m_ctx[l], norm_ffn[l][None], wu_bf[l], w_dw[l],
                     b_dw[l][None], wd_bf[l], ctx_len=ctx_len, tm=tm, fc=fc)
    return _final_norm(x_all, norm_final[None], ctx_len=ctx_len, tm=tm)
```

```python
import functools

import numpy as np
import jax
import jax.numpy as jnp
from jax import lax
from jax.experimental import pallas as pl
from jax.experimental.pallas import tpu as pltpu

HEAD_DIM = 64
A_HEADS = 8
A_KV_HEADS = 2
B_HEADS = 8
B_KV_HEADS = 2
C_HEADS = 4
C_DK = 128
C_DV = 128
N_BRANCH = 3
WINDOW = 128
BLOCK = 128
GRID_W = 64
ROPE_THETA = 10000.0
N_MOD = 6
EPS = 1e-6
NEG_BIG = -1e30
Q_WIDTH = A_HEADS * HEAD_DIM
KV_WIDTH = A_KV_HEADS * HEAD_DIM
C_WIDTH = C_HEADS * C_DK
FEAT_WIDTH = 2 * Q_WIDTH + 4 * KV_WIDTH + 5 * C_WIDTH
LANES = 128
SUBLANES = 8
VMEM_LIMIT = 56 * 1024 * 1024

F32 = jnp.float32
BF16 = jnp.bfloat16


def _cparams(*sem):
    return pltpu.CompilerParams(dimension_semantics=sem, vmem_limit_bytes=VMEM_LIMIT)


def _resident(shape):
    zeros = (0,) * len(shape)
    return pl.BlockSpec(shape, lambda *_: zeros, pipeline_mode=pl.Buffered(1))


def _dot(a, b):
    return jnp.dot(a, b, preferred_element_type=F32)


def _dot_nt(a, b):
    return lax.dot_general(a, b, (((1,), (1,)), ((), ())), preferred_element_type=F32)


def _dot_tn(a, b):
    return lax.dot_general(a, b, (((0,), (0,)), ((), ())), preferred_element_type=F32)


def _split3(x):
    hi = x.astype(BF16)
    r1 = x - hi.astype(F32)
    mid = r1.astype(BF16)
    lo = (r1 - mid.astype(F32)).astype(BF16)
    return hi, mid, lo


def _dot_exact_rhs(m3_bf16, x):
    return _dot(m3_bf16, jnp.concatenate(_split3(x), axis=0))


def _dot_exact_lhs(x, m_bf16):
    hi, mid, lo = _split3(x)
    return _dot(hi, m_bf16) + _dot(mid, m_bf16) + _dot(lo, m_bf16)


def _silu(x):
    return x * (1.0 / (1.0 + jnp.exp(-x)))


def _sigmoid(x):
    return 1.0 / (1.0 + jnp.exp(-x))


def _adaln(x, g, shift, scale):
    y = x * lax.rsqrt(jnp.mean(x * x, axis=-1, keepdims=True) + EPS)
    return (y * g) * (1.0 + scale) + shift


def _mod_rows(mlat_ref, mctx_ref, is_ctx):
    return jnp.where(is_ctx, mctx_ref[...], mlat_ref[0])


def _mod_kernel(c_ref, w_ref, b_ref, o_ref):
    sc = _silu(c_ref[...]).astype(BF16)
    o_ref[0] = _dot(sc, w_ref[0].astype(BF16)) + b_ref[0]


def _modulation(c_rows, w_mod, b_mod):
    depth, d, n = w_mod.shape
    rows = c_rows.shape[0]
    tn = d
    return pl.pallas_call(
        _mod_kernel,
        out_shape=jax.ShapeDtypeStruct((depth, rows, n), F32),
        grid=(depth, n // tn),
        in_specs=[pl.BlockSpec((rows, d), lambda l, j: (0, 0)),
                  pl.BlockSpec((1, d, tn), lambda l, j: (l, 0, j)),
                  pl.BlockSpec((1, 1, tn), lambda l, j: (l, 0, j))],
        out_specs=pl.BlockSpec((1, rows, tn), lambda l, j: (l, 0, j)),
        compiler_params=_cparams("parallel", "parallel"),
        name="modulation",
    )(c_rows, w_mod, b_mod.reshape(depth, 1, n))


def _rope_group(t, cos, sin_signed, first_half):
    partner = jnp.where(first_half, pltpu.roll(t, LANES - HEAD_DIM // 2, axis=1),
                        pltpu.roll(t, HEAD_DIM // 2, axis=1))
    return t * cos + partner * sin_signed


def _head_rms(t, gain, bd):
    ss = _dot_exact_lhs(t * t, bd)
    return t * lax.rsqrt(ss * (1.0 / HEAD_DIM) + EPS) * gain


def _forget_features(z, lb):
    e = jnp.exp(-jnp.abs(z))
    log_sig = jnp.minimum(z, 0.0) - jnp.log1p(e)
    log_f = log_sig + jnp.log1p(lb * jnp.exp(-z))
    inv = 1.0 / (1.0 + e)
    sig_neg = jnp.where(z >= 0.0, e * inv, inv)
    return (1.0 - lb) * sig_neg, log_f


def _proj_kernel(x_ref, mlat_ref, mctx_ref, g_ref, w_ref, cos_ref, sin_ref, qn_ref, kn_ref,
                 lbf_ref, lbb_ref, bd_ref,
                 qa_ref, qb_ref, kv_ref, hq_ref, hk_ref, hlf_ref, hv_ref, hog_ref, gt_ref,
                 *, ctx_len, tm, d_model):
    is_ctx = pl.program_id(1) * tm < ctx_len
    mod = _mod_rows(mlat_ref, mctx_ref, is_ctx)
    h = _adaln(x_ref[0], g_ref[...], mod[0:1], mod[1:2]).astype(BF16)
    cos = cos_ref[...]
    sin = sin_ref[...]
    bd = bd_ref[...]
    lane = lax.broadcasted_iota(jnp.int32, (tm, LANES), 1)
    first_half = (lane & (HEAD_DIM - 1)) < HEAD_DIM // 2
    scale = HEAD_DIM ** -0.5
    groups = Q_WIDTH // LANES

    def col(i):
        return _dot(h, w_ref[:, i * Q_WIDTH:(i + 1) * Q_WIDTH])

    def lanes(t, gidx):
        return t[:, gidx * LANES:(gidx + 1) * LANES]

    aq = col(0)
    qa_ref[0] = jnp.concatenate(
        [_rope_group(lanes(aq, gi), cos, sin, first_half) * scale for gi in range(groups)],
        axis=1).astype(BF16)
    bq = col(1)
    qb_ref[0] = jnp.concatenate(
        [_rope_group(_head_rms(lanes(bq, gi), qn_ref[...], bd), cos, sin, first_half) * scale
         for gi in range(groups)], axis=1).astype(BF16)
    kvp = col(2)
    kv_ref[0] = jnp.concatenate(
        [_rope_group(lanes(kvp, 0), cos, sin, first_half),
         _rope_group(_head_rms(lanes(kvp, 1), kn_ref[...], bd), cos, sin, first_half),
         lanes(kvp, 2), lanes(kvp, 3)], axis=1).astype(BF16)
    hq_ref[0] = _silu(col(3))
    kf, lff = _forget_features(col(4), lbf_ref[...])
    hk_ref[0, :, 0:C_WIDTH] = kf
    hlf_ref[0, :, 0:C_WIDTH] = lff
    kb, lfb = _forget_features(col(5), lbb_ref[...])
    hk_ref[0, :, C_WIDTH:2 * C_WIDTH] = kb
    hlf_ref[0, :, C_WIDTH:2 * C_WIDTH] = lfb
    hv_ref[0] = col(6)
    hog_ref[0] = _silu(col(7))
    for n in range(N_BRANCH):
        lo = FEAT_WIDTH + n * d_model
        gt_ref[0, :, n * d_model:(n + 1) * d_model] = _sigmoid(
            _dot(h, w_ref[:, lo:lo + d_model])).astype(BF16)


def _proj(x_all, m_lat, m_ctx, g, w, cos, sin, qn, kn, lbf, lbb, bd, *, ctx_len, tm):
    bsz, t, d = x_all.shape
    p = w.shape[1]
    row = lambda width: pl.BlockSpec((1, tm, width), lambda b, j: (b, j, 0))
    vec = lambda width: pl.BlockSpec((1, width), lambda b, j: (0, 0))
    outs = [(Q_WIDTH, BF16), (Q_WIDTH, BF16), (4 * KV_WIDTH, BF16), (C_WIDTH, F32),
            (2 * C_WIDTH, F32), (2 * C_WIDTH, F32), (C_WIDTH, F32), (C_WIDTH, F32),
            (N_BRANCH * d, BF16)]
    return pl.pallas_call(
        functools.partial(_proj_kernel, ctx_len=ctx_len, tm=tm, d_model=d),
        out_shape=[jax.ShapeDtypeStruct((bsz, t, wd), dt) for wd, dt in outs],
        grid=(bsz, t // tm),
        in_specs=[row(d),
                  pl.BlockSpec((1, N_MOD, d), lambda b, j: (b, 0, 0)),
                  pl.BlockSpec((N_MOD, d), lambda b, j: (0, 0)),
                  vec(d),
                  _resident((d, p)),
                  pl.BlockSpec((tm, LANES), lambda b, j: (j, 0)),
                  pl.BlockSpec((tm, LANES), lambda b, j: (j, 0)),
                  vec(LANES), vec(LANES), vec(C_WIDTH), vec(C_WIDTH),
                  pl.BlockSpec((LANES, LANES), lambda b, j: (0, 0))],
        out_specs=[row(wd) for wd, _ in outs],
        compiler_params=_cparams("parallel", "parallel"),
        name="proj",
    )(x_all, m_lat, m_ctx, g, w, cos, sin, qn, kn, lbf, lbb, bd)


def _scan_constants(chunk):
    idx = np.arange(chunk)
    mats = []
    for direction in (0, 1):
        tri = (idx[None, :] <= idx[:, None]) if direction == 0 else (idx[None, :] >= idx[:, None])
        tri = tri.astype(np.float32)
        blocks = [tri, 1.0 - tri]
        h = chunk // 2
        while h >= 1:
            start = (idx // (2 * h)) * (2 * h)
            ref = start + h - 1 if direction == 0 else start + h
            is_q = ((idx // h) % 2) != direction
            diff = tri - tri[ref]
            blocks.append(np.where(is_q[:, None], diff, -diff))
            h //= 2
        mats.append(np.tile(np.concatenate(blocks, axis=0), (1, 3)))
    return np.stack(mats)


def _scan_kernel(dmat_ref, qf_ref, kf_ref, lff_ref, vf_ref, qb_ref, kb_ref, lfb_ref, vb_ref,
                 of_ref, ob_ref, st_ref, *, chunk):
    @pl.when(pl.program_id(1) == 0)
    def _():
        st_ref[...] = jnp.zeros_like(st_ref)

    n_lev = chunk.bit_length() - 1
    dirs = (0, 1)
    q_all = (qf_ref[0], qb_ref[0])
    k_all = (kf_ref[0], kb_ref[0])
    v_all = (vf_ref[0].astype(BF16), vb_ref[0].astype(BF16))
    lf = (lff_ref[0], lfb_ref[0])
    e_all = [jnp.exp(_dot_exact_rhs(dmat_ref[d], lf[d])) for d in dirs]

    row = lax.broadcasted_iota(jnp.int32, (chunk, 1), 0)
    r2 = lax.broadcasted_iota(jnp.int32, (chunk, chunk), 0)
    c2 = lax.broadcasted_iota(jnp.int32, (chunk, chunk), 1)
    heads = [(d, hh) for hh in range(C_HEADS) for d in dirs]
    cols = lambda hh: slice(hh * C_DK, (hh + 1) * C_DK)

    scores = {}
    for d, hh in heads:
        sl = cols(hh)
        q, k = q_all[d][:, sl], k_all[d][:, sl]
        later = (r2 > c2) if d == 0 else (r2 < c2)
        sc = jnp.where(r2 == c2, _dot_nt(q.astype(BF16), k.astype(BF16)), 0.0)
        for lev in range(n_lev):
            shift = n_lev - 1 - lev
            is_q = ((row >> shift) & 1) != d
            m = (jnp.where(is_q, q, k)
                 * e_all[d][(2 + lev) * chunk:(3 + lev) * chunk, sl]).astype(BF16)
            pair = (later & ((r2 >> (shift + 1)) == (c2 >> (shift + 1)))
                    & ((r2 >> shift) != (c2 >> shift)))
            sc = jnp.where(pair, _dot_nt(m, m), sc)
        scores[d, hh] = sc.astype(BF16)

    inter = {}
    for d, hh in heads:
        sl = cols(hh)
        e_a = e_all[d][0:chunk, sl]
        e_b = e_all[d][chunk:2 * chunk, sl]
        st = st_ref[d, hh]
        inter[d, hh] = _dot_nt((q_all[d][:, sl] * e_a).astype(BF16), st.astype(BF16))
        kd = (k_all[d][:, sl] * e_b).astype(BF16)
        st_ref[d, hh] = st * (e_a[0:1] * e_b[0:1]) + _dot_tn(v_all[d][:, sl], kd)

    for d, o_ref in ((0, of_ref), (1, ob_ref)):
        o_ref[0] = jnp.concatenate(
            [inter[d, hh] + _dot(scores[d, hh], v_all[d][:, cols(hh)]) for hh in range(C_HEADS)],
            axis=1)


def _scan(dmats, hq, hk, hlf, hv, *, ctx_len, chunk):
    bsz, t, _ = hq.shape
    nc = t // chunk
    nctx = ctx_len // chunk

    def back(c):
        return jnp.where(c < nctx, nctx - 1 - c, nc - 1 + nctx - c)

    fwd = lambda col: pl.BlockSpec((1, chunk, C_WIDTH), lambda b, c: (b, c, col))
    bwd = lambda col: pl.BlockSpec((1, chunk, C_WIDTH), lambda b, c: (b, back(c), col))
    return pl.pallas_call(
        functools.partial(_scan_kernel, chunk=chunk),
        out_shape=[jax.ShapeDtypeStruct((bsz, t, C_WIDTH), F32)] * 2,
        grid=(bsz, nc),
        in_specs=[pl.BlockSpec(dmats.shape, lambda b, c: (0, 0, 0)),
                  fwd(0), fwd(0), fwd(0), fwd(0), bwd(0), bwd(1), bwd(1), bwd(0)],
        out_specs=[fwd(0), bwd(0)],
        scratch_shapes=[pltpu.VMEM((2, C_HEADS, C_DV, C_DK), F32)],
        compiler_params=_cparams("parallel", "arbitrary"),
        name="hgrn_scan",
    )(dmats, hq, hk, hlf, hv, hq, hk, hlf, hv)


def _stack_heads(q, kvh, group):
    return jnp.concatenate(
        [q[:, (kvh * group + g) * HEAD_DIM:(kvh * group + g + 1) * HEAD_DIM] for g in range(group)],
        axis=0)


def _attn_a_kernel(sink_ref, q_ref, kv_ref, o_ref, *, ctx_len, t_len):
    i = pl.program_id(1)
    n_ctx_blocks = ctx_len // BLOCK
    group = A_HEADS // A_KV_HEADS
    band = 3 * BLOCK
    q = q_ref[0]

    def sink_col(kvh):
        r = lax.broadcasted_iota(jnp.int32, (group * BLOCK, 1), 0)
        col = jnp.full((group * BLOCK, 1), sink_ref[kvh * group], F32)
        for g in range(1, group):
            col = jnp.where(r >= g * BLOCK, sink_ref[kvh * group + g], col)
        return col

    def ctx_kv(kvh):
        kc = kv_ref[0, 0:ctx_len, kvh * HEAD_DIM:(kvh + 1) * HEAD_DIM]
        vc = kv_ref[0, 0:ctx_len, 2 * KV_WIDTH + kvh * HEAD_DIM:2 * KV_WIDTH + (kvh + 1) * HEAD_DIM]
        return kc, vc

    def finish(parts):
        o_ref[0] = jnp.concatenate(
            [o[g * BLOCK:(g + 1) * BLOCK] for o in parts for g in range(group)], axis=1).astype(BF16)

    @pl.when(i < n_ctx_blocks)
    def _():
        parts = []
        for kvh in range(A_KV_HEADS):
            q4 = _stack_heads(q, kvh, group)
            kc, vc = ctx_kv(kvh)
            sk = sink_col(kvh)
            s = _dot_nt(q4, kc)
            m = jnp.maximum(jnp.max(s, axis=-1, keepdims=True), sk)
            p = jnp.exp(s - m)
            den = jnp.sum(p, axis=-1, keepdims=True) + jnp.exp(sk - m)
            parts.append(_dot(p.astype(BF16), vc) * (1.0 / den))
        finish(parts)

    @pl.when(i >= n_ctx_blocks)
    def _():
        n = i - n_ctx_blocks
        start = pl.multiple_of(jnp.minimum(ctx_len + (n - 1) * BLOCK, t_len - band), BLOCK)
        rq = lax.broadcasted_iota(jnp.int32, (group * BLOCK, band), 0)
        ck = lax.broadcasted_iota(jnp.int32, (group * BLOCK, band), 1)
        qpos = n * BLOCK + (rq & (BLOCK - 1))
        kpos = start - ctx_len + ck
        valid = (jnp.abs(kpos - qpos) <= WINDOW) & (kpos >= 0)
        parts = []
        for kvh in range(A_KV_HEADS):
            q4 = _stack_heads(q, kvh, group)
            kc, vc = ctx_kv(kvh)
            kl = kv_ref[0, pl.ds(start, band), kvh * HEAD_DIM:(kvh + 1) * HEAD_DIM]
            vl = kv_ref[0, pl.ds(start, band),
                        2 * KV_WIDTH + kvh * HEAD_DIM:2 * KV_WIDTH + (kvh + 1) * HEAD_DIM]
            sk = sink_col(kvh)
            s_ctx = _dot_nt(q4, kc)
            s_loc = jnp.where(valid, _dot_nt(q4, kl), NEG_BIG)
            m = jnp.maximum(jnp.maximum(jnp.max(s_ctx, axis=-1, keepdims=True),
                                        jnp.max(s_loc, axis=-1, keepdims=True)), sk)
            p_ctx = jnp.exp(s_ctx - m)
            p_loc = jnp.exp(s_loc - m)
            den = (jnp.sum(p_ctx, axis=-1, keepdims=True) + jnp.sum(p_loc, axis=-1, keepdims=True)
                   + jnp.exp(sk - m))
            o = _dot(p_ctx.astype(BF16), vc) + _dot(p_loc.astype(BF16), vl)
            parts.append(o * (1.0 / den))
        finish(parts)


def _attn_a(sink, qa, kv, *, ctx_len):
    bsz, t, _ = qa.shape
    return pl.pallas_call(
        functools.partial(_attn_a_kernel, ctx_len=ctx_len, t_len=t),
        out_shape=jax.ShapeDtypeStruct((bsz, t, Q_WIDTH), BF16),
        grid=(bsz, t // BLOCK),
        in_specs=[pl.BlockSpec(memory_space=pltpu.SMEM),
                  pl.BlockSpec((1, BLOCK, Q_WIDTH), lambda b, i: (b, i, 0)),
                  pl.BlockSpec((1, t, 4 * KV_WIDTH), lambda b, i: (b, 0, 0))],
        out_specs=pl.BlockSpec((1, BLOCK, Q_WIDTH), lambda b, i: (b, i, 0)),
        compiler_params=_cparams("parallel", "arbitrary"),
        name="attn_window",
    )(sink, qa, kv)


def _attn_b_kernel(q_ref, kv_ref, o_ref, vt_ref, *, ctx_len, t_len, kc):
    i = pl.program_id(1)
    group = B_HEADS // B_KV_HEADS
    row = lax.broadcasted_iota(jnp.int32, (LANES, 1), 0)

    @pl.when(i == 0)
    def _():
        vt_ref[...] = kv_ref[0, :, 3 * KV_WIDTH:4 * KV_WIDTH].astype(F32).T.astype(BF16)

    def run(n_keys):
        q = q_ref[0]
        heads = range(B_KV_HEADS)
        q4 = [_stack_heads(q, kvh, group) for kvh in heads]
        own = [(row >= kvh * HEAD_DIM) & (row < (kvh + 1) * HEAD_DIM) for kvh in heads]
        bounds = [(k0, min(k0 + kc, n_keys)) for k0 in range(0, n_keys, kc)]

        def scores(c, kvh):
            k0, k1 = bounds[c]
            k = kv_ref[0, k0:k1, KV_WIDTH + kvh * HEAD_DIM:KV_WIDTH + (kvh + 1) * HEAD_DIM]
            return _dot_nt(k, q4[kvh])

        m = [jnp.full((1, group * BLOCK), NEG_BIG, F32) for _ in heads]
        acc = [jnp.zeros((LANES, group * BLOCK), F32) for _ in heads]
        s = [scores(0, kvh) for kvh in heads]
        for c, (k0, k1) in enumerate(bounds):
            s_next = [scores(c + 1, kvh) for kvh in heads] if c + 1 < len(bounds) else None
            vt = vt_ref[:, k0:k1]
            for kvh in heads:
                m_new = jnp.maximum(m[kvh], jnp.max(s[kvh], axis=0, keepdims=True))
                p = jnp.exp(s[kvh] - m_new).astype(BF16)
                vt_ext = jnp.where(own[kvh], vt, jnp.ones_like(vt))
                acc[kvh] = acc[kvh] * jnp.exp(m[kvh] - m_new) + _dot(vt_ext, p)
                m[kvh] = m_new
            s = s_next
        pieces = []
        for kvh in heads:
            for g in range(group):
                blk = acc[kvh][:, g * BLOCK:(g + 1) * BLOCK].T
                o = blk * (1.0 / pltpu.roll(blk, HEAD_DIM, axis=1))
                pieces.append(o[:, kvh * HEAD_DIM:(kvh + 1) * HEAD_DIM])
        o_ref[0] = jnp.concatenate(pieces, axis=1).astype(BF16)

    @pl.when(i < ctx_len // BLOCK)
    def _():
        run(ctx_len)

    @pl.when(i >= ctx_len // BLOCK)
    def _():
        run(t_len)


def _attn_b(qb, kv, *, ctx_len):
    bsz, t, _ = qb.shape
    return pl.pallas_call(
        functools.partial(_attn_b_kernel, ctx_len=ctx_len, t_len=t, kc=9 * BLOCK),
        out_shape=jax.ShapeDtypeStruct((bsz, t, Q_WIDTH), BF16),
        grid=(bsz, t // BLOCK),
        in_specs=[pl.BlockSpec((1, BLOCK, Q_WIDTH), lambda b, i: (b, i, 0)),
                  pl.BlockSpec((1, t, 4 * KV_WIDTH), lambda b, i: (b, 0, 0))],
        out_specs=pl.BlockSpec((1, BLOCK, Q_WIDTH), lambda b, i: (b, i, 0)),
        scratch_shapes=[pltpu.VMEM((LANES, t), BF16)],
        compiler_params=_cparams("parallel", "arbitrary"),
        name="attn_global",
    )(qb, kv)


def _merge_kernel(x_ref, ya_ref, yb_ref, of_ref, ob_ref, og_ref, gt_ref, gn_ref, wb_ref, wo_ref,
                  mlat_ref, mctx_ref, o_ref, *, ctx_len, tm, d_model):
    is_ctx = pl.program_id(1) * tm < ctx_len
    mod = _mod_rows(mlat_ref, mctx_ref, is_ctx)
    oc = of_ref[0] + ob_ref[0]
    og = og_ref[0]
    gn = gn_ref[...]
    heads = []
    for hh in range(C_HEADS):
        sl = slice(hh * C_DV, (hh + 1) * C_DV)
        t = oc[:, sl]
        y = t * lax.rsqrt(jnp.mean(t * t, axis=-1, keepdims=True) + EPS) * gn
        heads.append(y * og[:, sl])
    yc = jnp.concatenate(heads, axis=1).astype(BF16)
    merged = None
    for n, y in enumerate((ya_ref[0], yb_ref[0], yc)):
        up = _dot(y, wb_ref[n])
        term = gt_ref[0, :, n * d_model:(n + 1) * d_model].astype(F32) * up
        merged = term if merged is None else merged + term
    out = _dot(merged.astype(BF16), wo_ref[...])
    o_ref[0] = x_ref[0] + mod[2:3] * out


def _merge(x_all, ya, yb, o_fwd, o_bwd, hog, gates, gn, wb, wo, m_lat, m_ctx, *, ctx_len, tm):
    bsz, t, d = x_all.shape
    row = lambda width: pl.BlockSpec((1, tm, width), lambda b, j: (b, j, 0))
    return pl.pallas_call(
        functools.partial(_merge_kernel, ctx_len=ctx_len, tm=tm, d_model=d),
        out_shape=jax.ShapeDtypeStruct((bsz, t, d), F32),
        grid=(bsz, t // tm),
        in_specs=[row(d), row(Q_WIDTH), row(Q_WIDTH), row(C_WIDTH), row(C_WIDTH),
                  row(C_WIDTH), row(N_BRANCH * d),
                  pl.BlockSpec((1, C_DV), lambda b, j: (0, 0)),
                  _resident(wb.shape), _resident(wo.shape),
                  pl.BlockSpec((1, N_MOD, d), lambda b, j: (b, 0, 0)),
                  pl.BlockSpec((N_MOD, d), lambda b, j: (0, 0))],
        out_specs=row(d),
        compiler_params=_cparams("parallel", "parallel"),
        name="merge",
    )(x_all, ya, yb, o_fwd, o_bwd, hog, gates, gn, wb, wo, m_lat, m_ctx)


def _ffn_kernel(x_ref, xp_ref, xn_ref, mlat_ref, mctx_ref, g_ref, wu_ref, wdw_ref, bdw_ref, wd_ref,
                o_ref, *, ctx_len, t_len, tm, d_ff, fc):
    j = pl.program_id(1)
    is_ctx = j * tm < ctx_len
    mod = _mod_rows(mlat_ref, mctx_ref, is_ctx)
    x = x_ref[0]
    xe = jnp.concatenate([xp_ref[0], x, xn_ref[0]], axis=0)
    h = _adaln(xe, g_ref[...], mod[3:4], mod[4:5]).astype(BF16)
    rows = tm + 2 * SUBLANES
    r = j * tm + lax.broadcasted_iota(jnp.int32, (tm, 1), 0)
    first = (r == 0) | (r == ctx_len)
    last = (r == ctx_len - 1) | (r == t_len - 1)

    def conv(lo):
        u = _dot(h, wu_ref[:, lo:lo + fc])
        prev = jnp.where(first, 0.0, pltpu.roll(u, 1, axis=0)[SUBLANES:SUBLANES + tm])
        nxt = jnp.where(last, 0.0, pltpu.roll(u, rows - 1, axis=0)[SUBLANES:SUBLANES + tm])
        w = wdw_ref[:, lo:lo + fc]
        return (w[0:1] * prev + w[1:2] * u[SUBLANES:SUBLANES + tm] + w[2:3] * nxt
                + bdw_ref[:, lo:lo + fc])

    acc = None
    for ci in range(d_ff // fc):
        a = conv(ci * fc)
        gate = conv(d_ff + ci * fc)
        act = (a * _silu(gate)).astype(BF16)
        part = _dot(act, wd_ref[ci * fc:(ci + 1) * fc, :])
        acc = part if acc is None else acc + part
    o_ref[0] = x + mod[5:6] * acc


def _ffn(x_mix, m_lat, m_ctx, g, wu, wdw, bdw, wd, *, ctx_len, tm, fc):
    bsz, t, d = x_mix.shape
    d_ff = wd.shape[0]
    per = tm // SUBLANES
    last_blk = t // SUBLANES - 1
    return pl.pallas_call(
        functools.partial(_ffn_kernel, ctx_len=ctx_len, t_len=t, tm=tm, d_ff=d_ff, fc=fc),
        out_shape=jax.ShapeDtypeStruct((bsz, t, d), F32),
        grid=(bsz, t // tm),
        in_specs=[pl.BlockSpec((1, tm, d), lambda b, j: (b, j, 0)),
                  pl.BlockSpec((1, SUBLANES, d), lambda b, j: (b, jnp.maximum(j * per - 1, 0), 0)),
                  pl.BlockSpec((1, SUBLANES, d),
                               lambda b, j: (b, jnp.minimum((j + 1) * per, last_blk), 0)),
                  pl.BlockSpec((1, N_MOD, d), lambda b, j: (b, 0, 0)),
                  pl.BlockSpec((N_MOD, d), lambda b, j: (0, 0)),
                  pl.BlockSpec((1, d), lambda b, j: (0, 0)),
                  _resident(wu.shape),
                  pl.BlockSpec(wdw.shape, lambda b, j: (0, 0)),
                  pl.BlockSpec(bdw.shape, lambda b, j: (0, 0)),
                  _resident(wd.shape)],
        out_specs=pl.BlockSpec((1, tm, d), lambda b, j: (b, j, 0)),
        compiler_params=_cparams("parallel", "parallel"),
        name="conv_ffn",
    )(x_mix, x_mix, x_mix, m_lat, m_ctx, g, wu, wdw, bdw, wd)


def _final_kernel(x_ref, g_ref, o_ref):
    x = x_ref[0]
    o_ref[0] = x * lax.rsqrt(jnp.mean(x * x, axis=-1, keepdims=True) + EPS) * g_ref[...]


def _final_norm(x_all, g, *, ctx_len, tm):
    bsz, t, d = x_all.shape
    off = ctx_len // tm
    return pl.pallas_call(
        _final_kernel,
        out_shape=jax.ShapeDtypeStruct((bsz, t - ctx_len, d), F32),
        grid=(bsz, (t - ctx_len) // tm),
        in_specs=[pl.BlockSpec((1, tm, d), lambda b, j: (b, j + off, 0)),
                  pl.BlockSpec((1, d), lambda b, j: (0, 0))],
        out_specs=pl.BlockSpec((1, tm, d), lambda b, j: (b, j, 0)),
        compiler_params=_cparams("parallel", "parallel"),
        name="final_norm",
    )(x_all, g)


def _half_split(n_heads):
    base = np.concatenate([np.arange(0, HEAD_DIM, 2), np.arange(1, HEAD_DIM, 2)])
    return np.concatenate([h * HEAD_DIM + base for h in range(n_heads)])


def _proj_column_order(d_model):
    sizes = (Q_WIDTH, KV_WIDTH, KV_WIDTH, Q_WIDTH, KV_WIDTH, KV_WIDTH) + (C_WIDTH,) * 5 \
        + (N_BRANCH * d_model,)
    off = np.concatenate([[0], np.cumsum(sizes)])
    aq, ak, av, bq, bk, bv = (int(o) for o in off[:6])
    plain = lambda o, n: o + np.arange(n)
    return np.concatenate([
        aq + _half_split(A_HEADS), bq + _half_split(B_HEADS),
        ak + _half_split(A_KV_HEADS), bk + _half_split(B_KV_HEADS),
        plain(av, KV_WIDTH), plain(bv, KV_WIDTH),
        plain(int(off[6]), int(off[-1] - off[6]))])


def _rope_tables(ctx_len, seq):
    rows = seq // GRID_W
    row = jnp.repeat(jnp.arange(rows), GRID_W).astype(F32)
    col = jnp.tile(jnp.arange(GRID_W), rows).astype(F32)
    half = HEAD_DIM // 2
    inv = ROPE_THETA ** (-jnp.arange(0, half, 2, dtype=F32) / half)
    ang = jnp.concatenate([row[:, None] * inv, col[:, None] * inv], axis=-1)
    cos = jnp.concatenate([jnp.ones((ctx_len, half), F32), jnp.cos(ang)], axis=0)
    sin = jnp.concatenate([jnp.zeros((ctx_len, half), F32), jnp.sin(ang)], axis=0)
    reps = LANES // HEAD_DIM
    return (jnp.tile(jnp.concatenate([cos, cos], axis=1), (1, reps)),
            jnp.tile(jnp.concatenate([-sin, sin], axis=1), (1, reps)))


def _lower_bounds(raw):
    p = jax.nn.softmax(raw.astype(F32), axis=0)
    return jnp.clip(jnp.cumsum(p, axis=0) - p[0], 0.0, 1.0)


def _row_tile(ctx_len):
    tm = 256
    while ctx_len % tm:
        tm //= 2
    return tm


def _ffn_chunk(d_ff):
    for fc in (1408, 1024, 512, 256, 128):
        if d_ff % fc == 0:
            return fc
    return d_ff


def kernel(x, c, ctx, c_ctx, w_mod, b_mod, norm_mix, norm_ffn, w_in, sink_a, qn_b, kn_b, lb_fwd,
           lb_bwd, gn_c, w_branch, w_out, w_up, w_dw, b_dw, w_down, norm_final):
    bsz, seq, d = x.shape
    ctx_len = ctx.shape[1]
    depth = w_mod.shape[0]
    assert ctx_len % BLOCK == 0 and seq % BLOCK == 0 and ctx_len >= BLOCK and seq >= 3 * BLOCK
    assert d % LANES == 0
    tm = _row_tile(ctx_len)
    chunk = 64
    fc =_ffn_chunk(w_down.shape[1])

    n_rows = -(-(bsz + 1) // SUBLANES) * SUBLANES
    c_rows = jnp.concatenate([c, c_ctx[None], jnp.zeros((n_rows - bsz - 1, d), F32)], axis=0)
    mods = _modulation(c_rows, w_mod, b_mod)
    m_lat = mods[:, :bsz].reshape(depth, bsz, N_MOD, d)
    m_ctx = mods[:, bsz].reshape(depth, N_MOD, d)

    order = _proj_column_order(d)
    w_in_p = jnp.take(w_in, order, axis=2).astype(BF16)
    wb_bf = w_branch.astype(BF16)
    wo_bf = w_out.astype(BF16)
    wu_bf = w_up.astype(BF16)
    wd_bf = w_down.astype(BF16)
    head_order = _half_split(1)
    qn_p = jnp.tile(qn_b[:, head_order], (1, LANES // HEAD_DIM))
    kn_p = jnp.tile(kn_b[:, head_order], (1, LANES // HEAD_DIM))
    lbf = _lower_bounds(lb_fwd)
    lbb = _lower_bounds(lb_bwd)
    cos, sin = _rope_tables(ctx_len, seq)
    lane = np.arange(LANES)
    bd = jnp.asarray(lane[:, None] // HEAD_DIM == lane[None, :] // HEAD_DIM, BF16)
    dmats = jnp.asarray(_scan_constants(chunk), BF16)

    x_all = jnp.concatenate([ctx, x], axis=1)
    for l in range(depth):
        qa, qb, kv, hq, hk, hlf, hv, hog, gates = _proj(
            x_all, m_lat[l], m_ctx[l], norm_mix[l][None], w_in_p[l], cos, sin,
            qn_p[l][None], kn_p[l][None], lbf[l][None], lbb[l][None], bd, ctx_len=ctx_len, tm=tm)
        o_fwd, o_bwd = _scan(dmats, hq, hk, hlf, hv, ctx_len=ctx_len, chunk=chunk)
        ya = _attn_a(sink_a[l], qa, kv, ctx_len=ctx_len)
        yb = _attn_b(qb, kv, ctx_len=ctx_len)
        x_mix = _merge(x_all, ya, yb, o_fwd, o_bwd, hog, gates, gn_c[l][None], wb_bf[l], wo_bf[l],
                       m_lat[l], m_ctx[l], ctx_len=ctx_len, tm=tm)
        x_all = _ffn(x_mix, m_lat[l], m_ctx[l], norm_ffn[l][None], wu_bf[l], w_dw[l],
                     b_dw[l][None], wd_bf[l], ctx_len=ctx_len, tm=tm, fc=fc)
    return _final_norm(x_all, norm_final[None], ctx_len=ctx_len, tm=tm)
```

```python
import functools

import numpy as np
import jax
import jax.numpy as jnp
from jax import lax
from jax.experimental import pallas as pl
from jax.experimental.pallas import tpu as pltpu

HEAD_DIM = 64
A_HEADS = 8
A_KV_HEADS = 2
B_HEADS = 8
B_KV_HEADS = 2
C_HEADS = 4
C_DK = 128
C_DV = 128
N_BRANCH = 3
WINDOW = 128
BLOCK = 128
GRID_W = 64
ROPE_THETA = 10000.0
N_MOD = 6
EPS = 1e-6
NEG_BIG = -1e30
Q_WIDTH = A_HEADS * HEAD_DIM
KV_WIDTH = A_KV_HEADS * HEAD_DIM
C_WIDTH = C_HEADS * C_DK
FEAT_WIDTH = 2 * Q_WIDTH + 4 * KV_WIDTH + 5 * C_WIDTH
LANES = 128
SUBLANES = 8
VMEM_LIMIT = 56 * 1024 * 1024

F32 = jnp.float32
BF16 = jnp.bfloat16


def _cparams(*sem):
    return pltpu.CompilerParams(dimension_semantics=sem, vmem_limit_bytes=VMEM_LIMIT)


def _resident(shape):
    zeros = (0,) * len(shape)
    return pl.BlockSpec(shape, lambda *_: zeros, pipeline_mode=pl.Buffered(1))


def _dot(a, b):
    return jnp.dot(a, b, preferred_element_type=F32)


def _dot_nt(a, b):
    return lax.dot_general(a, b, (((1,), (1,)), ((), ())), preferred_element_type=F32)


def _dot_tn(a, b):
    return lax.dot_general(a, b, (((0,), (0,)), ((), ())), preferred_element_type=F32)


def _split3(x):
    hi = x.astype(BF16)
    r1 = x - hi.astype(F32)
    mid = r1.astype(BF16)
    lo = (r1 - mid.astype(F32)).astype(BF16)
    return hi, mid, lo


def _dot_exact_rhs(m3_bf16, x):
    return _dot(m3_bf16, jnp.concatenate(_split3(x), axis=0))


def _dot_exact_lhs(x, m3_bf16):
    return _dot(jnp.concatenate(_split3(x), axis=1), m3_bf16)


def _silu(x):
    return x * (1.0 / (1.0 + jnp.exp(-x)))


def _sigmoid(x):
    return 1.0 / (1.0 + jnp.exp(-x))


def _adaln(x, g, shift, scale):
    y = x * lax.rsqrt(jnp.mean(x * x, axis=-1, keepdims=True) + EPS)
    return (y * g) * (1.0 + scale) + shift


def _mod_rows(mlat_ref, mctx_ref, is_ctx):
    return jnp.where(is_ctx, mctx_ref[...], mlat_ref[0])


def _mod_kernel(c_ref, w_ref, b_ref, o_ref):
    sc = _silu(c_ref[...]).astype(BF16)
    o_ref[0] = _dot(sc, w_ref[0].astype(BF16)) + b_ref[0]


def _modulation(c_rows, w_mod, b_mod):
    depth, d, n = w_mod.shape
    rows = c_rows.shape[0]
    tn = d
    return pl.pallas_call(
        _mod_kernel,
        out_shape=jax.ShapeDtypeStruct((depth, rows, n), F32),
        grid=(depth, n // tn),
        in_specs=[pl.BlockSpec((rows, d), lambda l, j: (0, 0)),
                  pl.BlockSpec((1, d, tn), lambda l, j: (l, 0, j)),
                  pl.BlockSpec((1, 1, tn), lambda l, j: (l, 0, j))],
        out_specs=pl.BlockSpec((1, rows, tn), lambda l, j: (l, 0, j)),
        compiler_params=_cparams("parallel", "parallel"),
        name="modulation",
    )(c_rows, w_mod, b_mod.reshape(depth, 1, n))


def _rope_group(t, cos, sin_signed, first_half):
    partner = jnp.where(first_half, pltpu.roll(t, LANES - HEAD_DIM // 2, axis=1),
                        pltpu.roll(t, HEAD_DIM // 2, axis=1))
    return t * cos + partner * sin_signed


def _head_rms(t, gain, bd):
    ss = _dot_exact_lhs(t * t, bd)
    return t * lax.rsqrt(ss * (1.0 / HEAD_DIM) + EPS) * gain


def _forget_features(z, lb):
    e = jnp.exp(-jnp.abs(z))
    inv = 1.0 / (1.0 + e)
    small = e * inv
    pos = z >= 0.0
    sig = jnp.where(pos, inv, small)
    sig_neg = jnp.where(pos, small, inv)
    return (1.0 - lb) * sig_neg, jnp.log(lb + (1.0 - lb) * sig)


def _proj_kernel(x_ref, mlat_ref, mctx_ref, g_ref, w_ref, cos_ref, sin_ref, qn_ref, kn_ref,
                 lbf_ref, lbb_ref, bd_ref,
                 qa_ref, qb_ref, kv_ref, hq_ref, hk_ref, hlf_ref, hv_ref, hog_ref, gt_ref,
                 *, ctx_len, tm, d_model):
    is_ctx = pl.program_id(1) * tm < ctx_len
    mod = _mod_rows(mlat_ref, mctx_ref, is_ctx)
    h = _adaln(x_ref[0], g_ref[...], mod[0:1], mod[1:2]).astype(BF16)
    cos = cos_ref[...]
    sin = sin_ref[...]
    bd = bd_ref[...]
    lane = lax.broadcasted_iota(jnp.int32, (tm, LANES), 1)
    first_half = (lane & (HEAD_DIM - 1)) < HEAD_DIM // 2
    scale = HEAD_DIM ** -0.5
    groups = Q_WIDTH // LANES

    def col(i):
        return _dot(h, w_ref[:, i * Q_WIDTH:(i + 1) * Q_WIDTH])

    def lanes(t, gidx):
        return t[:, gidx * LANES:(gidx + 1) * LANES]

    def gate(n):
        lo = FEAT_WIDTH + n * d_model
        gt_ref[0, :, n * d_model:(n + 1) * d_model] = _sigmoid(
            _dot(h, w_ref[:, lo:lo + d_model])).astype(BF16)

    kf, lff = _forget_features(col(4), lbf_ref[...])
    hk_ref[0, :, 0:C_WIDTH] = kf
    hlf_ref[0, :, 0:C_WIDTH] = lff
    gate(0)
    kb, lfb = _forget_features(col(5), lbb_ref[...])
    hk_ref[0, :, C_WIDTH:2 * C_WIDTH] = kb
    hlf_ref[0, :, C_WIDTH:2 * C_WIDTH] = lfb
    gate(1)
    bq = col(1)
    qb_ref[0] = jnp.concatenate(
        [_rope_group(_head_rms(lanes(bq, gi), qn_ref[...], bd), cos, sin, first_half) * scale
         for gi in range(groups)], axis=1).astype(BF16)
    hv_ref[0] = col(6)
    kvp = col(2)
    kv_ref[0] = jnp.concatenate(
        [_rope_group(lanes(kvp, 0), cos, sin, first_half),
         _rope_group(_head_rms(lanes(kvp, 1), kn_ref[...], bd), cos, sin, first_half),
         lanes(kvp, 2), lanes(kvp, 3)], axis=1).astype(BF16)
    gate(2)
    aq = col(0)
    qa_ref[0] = jnp.concatenate(
        [_rope_group(lanes(aq, gi), cos, sin, first_half) * scale for gi in range(groups)],
        axis=1).astype(BF16)
    hq_ref[0] = _silu(col(3))
    hog_ref[0] = _silu(col(7))


def _proj(x_all, m_lat, m_ctx, g, w, cos, sin, qn, kn, lbf, lbb, bd, *, ctx_len, tm):
    bsz, t, d = x_all.shape
    p = w.shape[1]
    row = lambda width: pl.BlockSpec((1, tm, width), lambda b, j: (b, j, 0))
    vec = lambda width: pl.BlockSpec((1, width), lambda b, j: (0, 0))
    outs = [(Q_WIDTH, BF16), (Q_WIDTH, BF16), (4 * KV_WIDTH, BF16), (C_WIDTH, F32),
            (2 * C_WIDTH, F32), (2 * C_WIDTH, F32), (C_WIDTH, F32), (C_WIDTH, F32),
            (N_BRANCH * d, BF16)]
    return pl.pallas_call(
        functools.partial(_proj_kernel, ctx_len=ctx_len, tm=tm, d_model=d),
        out_shape=[jax.ShapeDtypeStruct((bsz, t, wd), dt) for wd, dt in outs],
        grid=(bsz, t // tm),
        in_specs=[row(d),
                  pl.BlockSpec((1, N_MOD, d), lambda b, j: (b, 0, 0)),
                  pl.BlockSpec((N_MOD, d), lambda b, j: (0, 0)),
                  vec(d),
                  _resident((d, p)),
                  pl.BlockSpec((tm, LANES), lambda b, j: (j, 0)),
                  pl.BlockSpec((tm, LANES), lambda b, j: (j, 0)),
                  vec(LANES), vec(LANES), vec(C_WIDTH), vec(C_WIDTH),
                  pl.BlockSpec((3 * LANES, LANES), lambda b, j: (0, 0))],
        out_specs=[row(wd) for wd, _ in outs],
        compiler_params=_cparams("parallel", "parallel"),
        name="proj",
    )(x_all, m_lat, m_ctx, g, w, cos, sin, qn, kn, lbf, lbb, bd)


def _scan_constants(chunk):
    idx = np.arange(chunk)
    mats = []
    for direction in (0, 1):
        tri = (idx[None, :] <= idx[:, None]) if direction == 0 else (idx[None, :] >= idx[:, None])
        tri = tri.astype(np.float32)
        blocks = [tri, 1.0 - tri]
        h = chunk // 2
        while h >= 1:
            start = (idx // (2 * h)) * (2 * h)
            ref = start + h - 1 if direction == 0 else start + h
            is_q = ((idx // h) % 2) != direction
            diff = tri - tri[ref]
            blocks.append(np.where(is_q[:, None], diff, -diff))
            h //= 2
        mats.append(np.tile(np.concatenate(blocks, axis=0), (1, 3)))
    return np.stack(mats)


def _scan_kernel(dmat_ref, qf_ref, kf_ref, lff_ref, vf_ref, qb_ref, kb_ref, lfb_ref, vb_ref,
                 of_ref, ob_ref, st_ref, *, chunk, n_sub):
    @pl.when(pl.program_id(1) == 0)
    def _():
        st_ref[...] = jnp.zeros_like(st_ref)

    n_lev = chunk.bit_length() - 1
    dirs = (0, 1)
    rows = lambda sub: slice(sub * chunk, (sub + 1) * chunk)
    cols = lambda hh: slice(hh * C_DK, (hh + 1) * C_DK)
    q_all = (qf_ref[0], qb_ref[0])
    k_all = (kf_ref[0], kb_ref[0])
    v_all = (vf_ref[0].astype(BF16), vb_ref[0].astype(BF16))
    lf = (lff_ref[0], lfb_ref[0])
    e_all = {(d, sub): jnp.exp(_dot_exact_rhs(dmat_ref[d], lf[d][rows(sub)]))
             for sub in range(n_sub) for d in dirs}

    row = lax.broadcasted_iota(jnp.int32, (chunk, 1), 0)
    r2 = lax.broadcasted_iota(jnp.int32, (chunk, chunk), 0)
    c2 = lax.broadcasted_iota(jnp.int32, (chunk, chunk), 1)
    heads = [(d, hh) for hh in range(C_HEADS) for d in dirs]

    scores = {}
    for sub in range(n_sub):
        for d, hh in heads:
            sl = cols(hh)
            q, k = q_all[d][rows(sub), sl], k_all[d][rows(sub), sl]
            later = (r2 > c2) if d == 0 else (r2 < c2)
            sc = jnp.where(r2 == c2, _dot_nt(q.astype(BF16), k.astype(BF16)), 0.0)
            for lev in range(n_lev):
                shift = n_lev - 1 - lev
                is_q = ((row >> shift) & 1) != d
                m = (jnp.where(is_q, q, k)
                     * e_all[d, sub][(2 + lev) * chunk:(3 + lev) * chunk, sl]).astype(BF16)
                pair = (later & ((r2 >> (shift + 1)) == (c2 >> (shift + 1)))
                        & ((r2 >> shift) != (c2 >> shift)))
                sc = jnp.where(pair, _dot_nt(m, m), sc)
            scores[d, sub, hh] = sc.astype(BF16)

    inter = {}
    for d, hh in heads:
        sl = cols(hh)
        st = st_ref[d, hh]
        for sub in (range(n_sub) if d == 0 else reversed(range(n_sub))):
            e_a = e_all[d, sub][0:chunk, sl]
            e_b = e_all[d, sub][chunk:2 * chunk, sl]
            inter[d, sub, hh] = _dot_nt((q_all[d][rows(sub), sl] * e_a).astype(BF16),
                                        st.astype(BF16))
            kd = (k_all[d][rows(sub), sl] * e_b).astype(BF16)
            st = st * (e_a[0:1] * e_b[0:1]) + _dot_tn(v_all[d][rows(sub), sl], kd)
        st_ref[d, hh] = st

    for d, o_ref in ((0, of_ref), (1, ob_ref)):
        for sub in range(n_sub):
            o_ref[0, rows(sub)] = jnp.concatenate(
                [inter[d, sub, hh] + _dot(scores[d, sub, hh], v_all[d][rows(sub), cols(hh)])
                 for hh in range(C_HEADS)], axis=1)


def _scan(dmats, hq, hk, hlf, hv, *, ctx_len, chunk, n_sub):
    bsz, t, _ = hq.shape
    blk = chunk * n_sub
    nc = t // blk
    nctx = ctx_len // blk

    def back(c):
        return jnp.where(c < nctx, nctx - 1 - c, nc - 1 + nctx - c)

    fwd = lambda col: pl.BlockSpec((1, blk, C_WIDTH), lambda b, c: (b, c, col))
    bwd = lambda col: pl.BlockSpec((1, blk, C_WIDTH), lambda b, c: (b, back(c), col))
    return pl.pallas_call(
        functools.partial(_scan_kernel, chunk=chunk, n_sub=n_sub),
        out_shape=[jax.ShapeDtypeStruct((bsz, t, C_WIDTH), F32)] * 2,
        grid=(bsz, nc),
        in_specs=[pl.BlockSpec(dmats.shape, lambda b, c: (0, 0, 0)),
                  fwd(0), fwd(0), fwd(0), fwd(0), bwd(0), bwd(1), bwd(1), bwd(0)],
        out_specs=[fwd(0), bwd(0)],
        scratch_shapes=[pltpu.VMEM((2, C_HEADS, C_DV, C_DK), F32)],
        compiler_params=_cparams("parallel", "arbitrary"),
        name="hgrn_scan",
    )(dmats, hq, hk, hlf, hv, hq, hk, hlf, hv)


def _stack_heads(q, kvh, group):
    return jnp.concatenate(
        [q[:, (kvh * group + g) * HEAD_DIM:(kvh * group + g + 1) * HEAD_DIM] for g in range(group)],
        axis=0)


def _unstack_heads(accs, group):
    pieces = []
    for kvh, acc in enumerate(accs):
        for g in range(group):
            blk = acc[:, g * BLOCK:(g + 1) * BLOCK].T
            o = blk * (1.0 / pltpu.roll(blk, HEAD_DIM, axis=1))
            pieces.append(o[:, kvh * HEAD_DIM:(kvh + 1) * HEAD_DIM])
    return jnp.concatenate(pieces, axis=1)


def _attn_a_kernel(sink_ref, q_ref, kv_ref, o_ref, *, ctx_len, t_len):
    i = pl.program_id(1)
    n_ctx_blocks = ctx_len // BLOCK
    group = A_HEADS // A_KV_HEADS
    band = 3 * BLOCK
    q = q_ref[0]

    def sink_col(kvh):
        r = lax.broadcasted_iota(jnp.int32, (group * BLOCK, 1), 0)
        col = jnp.full((group * BLOCK, 1), sink_ref[kvh * group], F32)
        for g in range(1, group):
            col = jnp.where(r >= g * BLOCK, sink_ref[kvh * group + g], col)
        return col

    def ctx_kv(kvh):
        kc = kv_ref[0, 0:ctx_len, kvh * HEAD_DIM:(kvh + 1) * HEAD_DIM]
        vc = kv_ref[0, 0:ctx_len, 2 * KV_WIDTH + kvh * HEAD_DIM:2 * KV_WIDTH + (kvh + 1) * HEAD_DIM]
        return kc, vc

    def finish(parts):
        o_ref[0] = jnp.concatenate(
            [o[g * BLOCK:(g + 1) * BLOCK] for o in parts for g in range(group)], axis=1).astype(BF16)

    @pl.when(i < n_ctx_blocks)
    def _():
        parts = []
        for kvh in range(A_KV_HEADS):
            q4 = _stack_heads(q, kvh, group)
            kc, vc = ctx_kv(kvh)
            sk = sink_col(kvh)
            s = _dot_nt(q4, kc)
            m = jnp.maximum(jnp.max(s, axis=-1, keepdims=True), sk)
            p = jnp.exp(s - m)
            den = jnp.sum(p, axis=-1, keepdims=True) + jnp.exp(sk - m)
            parts.append(_dot(p.astype(BF16), vc) * (1.0 / den))
        finish(parts)

    @pl.when(i >= n_ctx_blocks)
    def _():
        n = i - n_ctx_blocks
        start = pl.multiple_of(jnp.minimum(ctx_len + (n - 1) * BLOCK, t_len - band), BLOCK)
        rq = lax.broadcasted_iota(jnp.int32, (group * BLOCK, band), 0)
        ck = lax.broadcasted_iota(jnp.int32, (group * BLOCK, band), 1)
        qpos = n * BLOCK + (rq & (BLOCK - 1))
        kpos = start - ctx_len + ck
        valid = (jnp.abs(kpos - qpos) <= WINDOW) & (kpos >= 0)
        parts = []
        for kvh in range(A_KV_HEADS):
            q4 = _stack_heads(q, kvh, group)
            kc, vc = ctx_kv(kvh)
            kl = kv_ref[0, pl.ds(start, band), kvh * HEAD_DIM:(kvh + 1) * HEAD_DIM]
            vl = kv_ref[0, pl.ds(start, band),
                        2 * KV_WIDTH + kvh * HEAD_DIM:2 * KV_WIDTH + (kvh + 1) * HEAD_DIM]
            sk = sink_col(kvh)
            s_ctx = _dot_nt(q4, kc)
            s_loc = jnp.where(valid, _dot_nt(q4, kl), NEG_BIG)
            m = jnp.maximum(jnp.maximum(jnp.max(s_ctx, axis=-1, keepdims=True),
                                        jnp.max(s_loc, axis=-1, keepdims=True)), sk)
            p_ctx = jnp.exp(s_ctx - m)
            p_loc = jnp.exp(s_loc - m)
            den = (jnp.sum(p_ctx, axis=-1, keepdims=True) + jnp.sum(p_loc, axis=-1, keepdims=True)
                   + jnp.exp(sk - m))
            o = _dot(p_ctx.astype(BF16), vc) + _dot(p_loc.astype(BF16), vl)
            parts.append(o * (1.0 / den))
        finish(parts)


def _attn_a(sink, qa, kv, *, ctx_len):
    bsz, t, _ = qa.shape
    return pl.pallas_call(
        functools.partial(_attn_a_kernel, ctx_len=ctx_len, t_len=t),
        out_shape=jax.ShapeDtypeStruct((bsz, t, Q_WIDTH), BF16),
        grid=(bsz, t // BLOCK),
        in_specs=[pl.BlockSpec(memory_space=pltpu.SMEM),
                  pl.BlockSpec((1, BLOCK, Q_WIDTH), lambda b, i: (b, i, 0)),
                  pl.BlockSpec((1, t, 4 * KV_WIDTH), lambda b, i: (b, 0, 0))],
        out_specs=pl.BlockSpec((1, BLOCK, Q_WIDTH), lambda b, i: (b, i, 0)),
        compiler_params=_cparams("parallel", "arbitrary"),
        name="attn_window",
    )(sink, qa, kv)


def _attn_b_kernel(q_ref, kv_ref, o_ref, vt_ref, *, ctx_len, t_len, kc):
    i = pl.program_id(1)
    group = B_HEADS // B_KV_HEADS
    row = lax.broadcasted_iota(jnp.int32, (LANES, 1), 0)

    @pl.when(i == 0)
    def _():
        vt_ref[...] = kv_ref[0, :, 3 * KV_WIDTH:4 * KV_WIDTH].astype(F32).T.astype(BF16)

    def run(n_keys):
        q = q_ref[0]
        heads = range(B_KV_HEADS)
        q4 = [_stack_heads(q, kvh, group) for kvh in heads]
        own = [(row >= kvh * HEAD_DIM) & (row < (kvh + 1) * HEAD_DIM) for kvh in heads]
        bounds = [(k0, min(k0 + kc, n_keys)) for k0 in range(0, n_keys, kc)]

        def scores(c, kvh):
            k0, k1 = bounds[c]
            k = kv_ref[0, k0:k1, KV_WIDTH + kvh * HEAD_DIM:KV_WIDTH + (kvh + 1) * HEAD_DIM]
            return _dot_nt(k, q4[kvh])

        m = [jnp.full((1, group * BLOCK), NEG_BIG, F32) for _ in heads]
        acc = [jnp.zeros((LANES, group * BLOCK), F32) for _ in heads]
        s = [scores(0, kvh) for kvh in heads]
        for c, (k0, k1) in enumerate(bounds):
            s_next = [scores(c + 1, kvh) for kvh in heads] if c + 1 < len(bounds) else None
            vt = vt_ref[:, k0:k1]
            for kvh in heads:
                m_new = jnp.maximum(m[kvh], jnp.max(s[kvh], axis=0, keepdims=True))
                p = jnp.exp(s[kvh] - m_new).astype(BF16)
                vt_ext = jnp.where(own[kvh], vt, jnp.ones_like(vt))
                acc[kvh] = acc[kvh] * jnp.exp(m[kvh] - m_new) + _dot(vt_ext, p)
                m[kvh] = m_new
            s = s_next
        o_ref[0] = _unstack_heads(acc, group).astype(BF16)

    @pl.when(i < ctx_len // BLOCK)
    def _():
        run(ctx_len)

    @pl.when(i >= ctx_len // BLOCK)
    def _():
        run(t_len)


def _attn_b(qb, kv, *, ctx_len):
    bsz, t, _ = qb.shape
    return pl.pallas_call(
        functools.partial(_attn_b_kernel, ctx_len=ctx_len, t_len=t, kc=9 * BLOCK),
        out_shape=jax.ShapeDtypeStruct((bsz, t, Q_WIDTH), BF16),
        grid=(bsz, t // BLOCK),
        in_specs=[pl.BlockSpec((1, BLOCK, Q_WIDTH), lambda b, i: (b, i, 0)),
                  pl.BlockSpec((1, t, 4 * KV_WIDTH), lambda b, i: (b, 0, 0))],
        out_specs=pl.BlockSpec((1, BLOCK, Q_WIDTH), lambda b, i: (b, i, 0)),
        scratch_shapes=[pltpu.VMEM((LANES, t), BF16)],
        compiler_params=_cparams("parallel", "arbitrary"),
        name="attn_global",
    )(qb, kv)


def _merge_kernel(x_ref, ya_ref, yb_ref, of_ref, ob_ref, og_ref, gt_ref, gn_ref, wb_ref, wo_ref,
                  mlat_ref, mctx_ref, o_ref, *, ctx_len, tm, d_model):
    is_ctx = pl.program_id(1) * tm < ctx_len
    mod = _mod_rows(mlat_ref, mctx_ref, is_ctx)
    oc = of_ref[0] + ob_ref[0]
    og = og_ref[0]
    gn = gn_ref[...]
    heads = []
    for hh in range(C_HEADS):
        sl = slice(hh * C_DV, (hh + 1) * C_DV)
        t = oc[:, sl]
        y = t * lax.rsqrt(jnp.mean(t * t, axis=-1, keepdims=True) + EPS) * gn
        heads.append(y * og[:, sl])
    yc = jnp.concatenate(heads, axis=1).astype(BF16)
    merged = None
    for n, y in enumerate((ya_ref[0], yb_ref[0], yc)):
        up = _dot(y, wb_ref[n])
        term = gt_ref[0, :, n * d_model:(n + 1) * d_model].astype(F32) * up
        merged = term if merged is None else merged + term
    out = _dot(merged.astype(BF16), wo_ref[...])
    o_ref[0] = x_ref[0] + mod[2:3] * out


def _merge(x_all, ya, yb, o_fwd, o_bwd, hog, gates, gn, wb, wo, m_lat, m_ctx, *, ctx_len, tm):
    bsz, t, d = x_all.shape
    row = lambda width: pl.BlockSpec((1, tm, width), lambda b, j: (b, j, 0))
    return pl.pallas_call(
        functools.partial(_merge_kernel, ctx_len=ctx_len, tm=tm, d_model=d),
        out_shape=jax.ShapeDtypeStruct((bsz, t, d), F32),
        grid=(bsz, t // tm),
        in_specs=[row(d), row(Q_WIDTH), row(Q_WIDTH), row(C_WIDTH), row(C_WIDTH),
                  row(C_WIDTH), row(N_BRANCH * d),
                  pl.BlockSpec((1, C_DV), lambda b, j: (0, 0)),
                  _resident(wb.shape), _resident(wo.shape),
                  pl.BlockSpec((1, N_MOD, d), lambda b, j: (b, 0, 0)),
                  pl.BlockSpec((N_MOD, d), lambda b, j: (0, 0))],
        out_specs=row(d),
        compiler_params=_cparams("parallel", "parallel"),
        name="merge",
    )(x_all, ya, yb, o_fwd, o_bwd, hog, gates, gn, wb, wo, m_lat, m_ctx)


def _ffn_kernel(x_ref, xp_ref, xn_ref, mlat_ref, mctx_ref, g_ref, wu_ref, wdw_ref, bdw_ref, wd_ref,
                o_ref, *, ctx_len, t_len, tm, d_ff, fc, down_group):
    r0 = pl.program_id(1) * tm
    mod = _mod_rows(mlat_ref, mctx_ref, r0 < ctx_len)
    x = x_ref[0]
    norm = lambda rows_: _adaln(rows_, g_ref[...], mod[3:4], mod[4:5])
    keep_prev = jnp.where((r0 == 0) | (r0 == ctx_len), 0.0, 1.0)
    keep_next = jnp.where((r0 + tm == ctx_len) | (r0 + tm == t_len), 0.0, 1.0)
    h = jnp.concatenate([norm(xp_ref[0]) * keep_prev, norm(x), norm(xn_ref[0]) * keep_next],
                        axis=0).astype(BF16)
    rows = tm + 2 * SUBLANES
    mid = slice(SUBLANES, SUBLANES + tm)

    def up(ci):
        return (_dot(h, wu_ref[:, ci * fc:(ci + 1) * fc]),
                _dot(h, wu_ref[:, d_ff + ci * fc:d_ff + (ci + 1) * fc]))

    def conv(u, lo):
        w = wdw_ref[:, lo:lo + fc]
        return (w[0:1] * pltpu.roll(u, 1, axis=0)[mid] + w[1:2] * u[mid]
                + w[2:3] * pltpu.roll(u, rows - 1, axis=0)[mid] + bdw_ref[:, lo:lo + fc])

    n_chunks = d_ff // fc
    ups = [up(0)]
    acts, acc, k0 = [], None, 0
    for ci in range(n_chunks):
        if ci + 1 < n_chunks:
            ups.append(up(ci + 1))
        ua, ug = ups[ci]
        acts.append((conv(ua, ci * fc) * _silu(conv(ug, d_ff + ci * fc))).astype(BF16))
        if len(acts) == down_group or ci + 1 == n_chunks:
            k1 = (ci + 1) * fc
            part = _dot(jnp.concatenate(acts, axis=1), wd_ref[k0:k1, :])
            acc = part if acc is None else acc + part
            acts, k0 = [], k1
    o_ref[0] = x + mod[5:6] * acc


def _ffn(x_mix, m_lat, m_ctx, g, wu, wdw, bdw, wd, *, ctx_len, tm, fc):
    bsz, t, d = x_mix.shape
    d_ff = wd.shape[0]
    per = tm // SUBLANES
    last_blk = t // SUBLANES - 1
    return pl.pallas_call(
        functools.partial(_ffn_kernel, ctx_len=ctx_len, t_len=t, tm=tm, d_ff=d_ff, fc=fc,
                          down_group=4),
        out_shape=jax.ShapeDtypeStruct((bsz, t, d), F32),
        grid=(bsz, t // tm),
        in_specs=[pl.BlockSpec((1, tm, d), lambda b, j: (b, j, 0)),
                  pl.BlockSpec((1, SUBLANES, d), lambda b, j: (b, jnp.maximum(j * per - 1, 0), 0)),
                  pl.BlockSpec((1, SUBLANES, d),
                               lambda b, j: (b, jnp.minimum((j + 1) * per, last_blk), 0)),
                  pl.BlockSpec((1, N_MOD, d), lambda b, j: (b, 0, 0)),
                  pl.BlockSpec((N_MOD, d), lambda b, j: (0, 0)),
                  pl.BlockSpec((1, d), lambda b, j: (0, 0)),
                  _resident(wu.shape),
                  pl.BlockSpec(wdw.shape, lambda b, j: (0, 0)),
                  pl.BlockSpec(bdw.shape, lambda b, j: (0, 0)),
                  _resident(wd.shape)],
        out_specs=pl.BlockSpec((1, tm, d), lambda b, j: (b, j, 0)),
        compiler_params=_cparams("parallel", "parallel"),
        name="conv_ffn",
    )(x_mix, x_mix, x_mix, m_lat, m_ctx, g, wu, wdw, bdw, wd)


def _final_kernel(x_ref, g_ref, o_ref):
    x = x_ref[0]
    o_ref[0] = x * lax.rsqrt(jnp.mean(x * x, axis=-1, keepdims=True) + EPS) * g_ref[...]


def _final_norm(x_all, g, *, ctx_len, tm):
    bsz, t, d = x_all.shape
    off = ctx_len // tm
    return pl.pallas_call(
        _final_kernel,
        out_shape=jax.ShapeDtypeStruct((bsz, t - ctx_len, d), F32),
        grid=(bsz, (t - ctx_len) // tm),
        in_specs=[pl.BlockSpec((1, tm, d), lambda b, j: (b, j + off, 0)),
                  pl.BlockSpec((1, d), lambda b, j: (0, 0))],
        out_specs=pl.BlockSpec((1, tm, d), lambda b, j: (b, j, 0)),
        compiler_params=_cparams("parallel", "parallel"),
        name="final_norm",
    )(x_all, g)


def _half_split(n_heads):
    base = np.concatenate([np.arange(0, HEAD_DIM, 2), np.arange(1, HEAD_DIM, 2)])
    return np.concatenate([h * HEAD_DIM + base for h in range(n_heads)])


def _proj_column_order(d_model):
    sizes = (Q_WIDTH, KV_WIDTH, KV_WIDTH, Q_WIDTH, KV_WIDTH, KV_WIDTH) + (C_WIDTH,) * 5 \
        + (N_BRANCH * d_model,)
    off = np.concatenate([[0], np.cumsum(sizes)])
    aq, ak, av, bq, bk, bv = (int(o) for o in off[:6])
    plain = lambda o, n: o + np.arange(n)
    return np.concatenate([
        aq + _half_split(A_HEADS), bq + _half_split(B_HEADS),
        ak + _half_split(A_KV_HEADS), bk + _half_split(B_KV_HEADS),
        plain(av, KV_WIDTH), plain(bv, KV_WIDTH),
        plain(int(off[6]), int(off[-1] - off[6]))])


def _rope_tables(ctx_len, seq):
    rows = seq // GRID_W
    row = jnp.repeat(jnp.arange(rows), GRID_W).astype(F32)
    col = jnp.tile(jnp.arange(GRID_W), rows).astype(F32)
    half = HEAD_DIM // 2
    inv = ROPE_THETA ** (-jnp.arange(0, half, 2, dtype=F32) / half)
    ang = jnp.concatenate([row[:, None] * inv, col[:, None] * inv], axis=-1)
    cos = jnp.concatenate([jnp.ones((ctx_len, half), F32), jnp.cos(ang)], axis=0)
    sin = jnp.concatenate([jnp.zeros((ctx_len, half), F32), jnp.sin(ang)], axis=0)
    reps = LANES // HEAD_DIM
    return (jnp.tile(jnp.concatenate([cos, cos], axis=1), (1, reps)),
            jnp.tile(jnp.concatenate([-sin, sin], axis=1), (1, reps)))


def _lower_bounds(raw):
    p = jax.nn.softmax(raw.astype(F32), axis=0)
    return jnp.clip(jnp.cumsum(p, axis=0) - p[0], 0.0, 1.0)


def _row_tile(ctx_len):
    tm = 256
    while ctx_len % tm:
        tm //= 2
    return tm


def _ffn_chunk(d_ff):
    for fc in (256, 128):
        if d_ff % fc == 0:
            return fc
    return d_ff


def kernel(x, c, ctx, c_ctx, w_mod, b_mod, norm_mix, norm_ffn, w_in, sink_a, qn_b, kn_b, lb_fwd,
           lb_bwd, gn_c, w_branch, w_out, w_up, w_dw, b_dw, w_down, norm_final):
    bsz, seq, d = x.shape
    ctx_len = ctx.shape[1]
    depth = w_mod.shape[0]
    assert ctx_len % BLOCK == 0 and seq % BLOCK == 0 and ctx_len >= BLOCK and seq >= 3 * BLOCK
    assert d % LANES == 0
    tm = _row_tile(ctx_len)
    chunk = 64
    fc =_ffn_chunk(w_down.shape[1])

    n_rows = -(-(bsz + 1) // SUBLANES) * SUBLANES
    c_rows = jnp.concatenate([c, c_ctx[None], jnp.zeros((n_rows - bsz - 1, d), F32)], axis=0)
    mods = _modulation(c_rows, w_mod, b_mod)
    m_lat = mods[:, :bsz].reshape(depth, bsz, N_MOD, d)
    m_ctx = mods[:, bsz].reshape(depth, N_MOD, d)

    order = _proj_column_order(d)
    w_in_p = jnp.take(w_in, order, axis=2).astype(BF16)
    wb_bf = w_branch.astype(BF16)
    wo_bf = w_out.astype(BF16)
    wu_bf = w_up.astype(BF16)
    wd_bf = w_down.astype(BF16)
    head_order = _half_split(1)
    qn_p = jnp.tile(qn_b[:, head_order], (1, LANES // HEAD_DIM))
    kn_p = jnp.tile(kn_b[:, head_order], (1, LANES // HEAD_DIM))
    lbf = _lower_bounds(lb_fwd)
    lbb = _lower_bounds(lb_bwd)
    cos, sin = _rope_tables(ctx_len, seq)
    lane = np.arange(LANES)
    bd = jnp.asarray(np.tile(lane[:, None] // HEAD_DIM == lane[None, :] // HEAD_DIM, (3, 1)), BF16)
    dmats = jnp.asarray(_scan_constants(chunk), BF16)

    x_all = jnp.concatenate([ctx, x], axis=1)
    for l in range(depth):
        qa, qb, kv, hq, hk, hlf, hv, hog, gates = _proj(
            x_all, m_lat[l], m_ctx[l], norm_mix[l][None], w_in_p[l], cos, sin,
            qn_p[l][None], kn_p[l][None], lbf[l][None], lbb[l][None], bd, ctx_len=ctx_len, tm=tm)
        o_fwd, o_bwd = _scan(dmats, hq, hk, hlf, hv, ctx_len=ctx_len, chunk=chunk, n_sub=2)
        ya = _attn_a(sink_a[l], qa, kv, ctx_len=ctx_len)
        yb = _attn_b(qb, kv, ctx_len=ctx_len)
        x_mix = _merge(x_all, ya, yb, o_fwd, o_bwd, hog, gates, gn_c[l][None], wb_bf[l], wo_bf[l],
                       m_lat[l], m_ctx[l], ctx_len=ctx_len, tm=tm)
        x_all = _ffn(x_mix, m_lat[l], m_ctx[l], norm_ffn[l][None], wu_bf[l], w_dw[l],
                     b_dw[l][None], wd_bf[l], ctx_len=ctx_len, tm=tm, fc=fc)
    return _final_norm(x_all, norm_final[None], ctx_len=ctx_len, tm=tm)
```

```python
import functools

import numpy as np
import jax
import jax.numpy as jnp
from jax import lax
from jax.experimental import pallas as pl
from jax.experimental.pallas import tpu as pltpu

HEAD_DIM = 64
A_HEADS = 8
A_KV_HEADS = 2
B_HEADS = 8
B_KV_HEADS = 2
C_HEADS = 4
C_DK = 128
C_DV = 128
N_BRANCH = 3
WINDOW = 128
BLOCK = 128
GRID_W = 64
ROPE_THETA = 10000.0
N_MOD = 6
EPS = 1e-6
LOG2_E = 1.4426950408889634
NEG_BIG = -1e30
Q_WIDTH = A_HEADS * HEAD_DIM
KV_WIDTH = A_KV_HEADS * HEAD_DIM
C_WIDTH = C_HEADS * C_DK
FEAT_WIDTH = 2 * Q_WIDTH + 4 * KV_WIDTH + 5 * C_WIDTH
LANES = 128
SUBLANES = 8
VMEM_LIMIT = 56 * 1024 * 1024

F32 = jnp.float32
BF16 = jnp.bfloat16


def _cparams(*sem):
    return pltpu.CompilerParams(dimension_semantics=sem, vmem_limit_bytes=VMEM_LIMIT)


def _resident(shape):
    zeros = (0,) * len(shape)
    return pl.BlockSpec(shape, lambda *_: zeros, pipeline_mode=pl.Buffered(1))


def _dot(a, b):
    return jnp.dot(a, b, preferred_element_type=F32)


def _dot_nt(a, b):
    return lax.dot_general(a, b, (((1,), (1,)), ((), ())), preferred_element_type=F32)


def _dot_tn(a, b):
    return lax.dot_general(a, b, (((0,), (0,)), ((), ())), preferred_element_type=F32)


def _split3(x):
    hi = x.astype(BF16)
    r1 = x - hi.astype(F32)
    mid = r1.astype(BF16)
    lo = (r1 - mid.astype(F32)).astype(BF16)
    return hi, mid, lo


def _dot_exact_rhs(m3_bf16, x):
    return _dot(m3_bf16, jnp.concatenate(_split3(x), axis=0))


def _dot_exact_lhs(x, m3_bf16):
    return _dot(jnp.concatenate(_split3(x), axis=1), m3_bf16)


def _silu(x):
    return x * (1.0 / (1.0 + jnp.exp(-x)))


def _sigmoid(x):
    return 1.0 / (1.0 + jnp.exp(-x))


def _adaln(x, g, shift, scale):
    y = x * lax.rsqrt(jnp.mean(x * x, axis=-1, keepdims=True) + EPS)
    return (y * g) * (1.0 + scale) + shift


def _mod_rows(mlat_ref, mctx_ref, is_ctx):
    return jnp.where(is_ctx, mctx_ref[...], mlat_ref[0])


def _mod_picker(mlat_ref, mctx_ref, r0, tm, ctx_len):
    if ctx_len % tm == 0:
        mod = _mod_rows(mlat_ref, mctx_ref, r0 < ctx_len)
        return lambda i: mod[i:i + 1]
    is_ctx = r0 + lax.broadcasted_iota(jnp.int32, (tm, 1), 0) < ctx_len
    return lambda i: jnp.where(is_ctx, mctx_ref[i:i + 1], mlat_ref[0, i:i + 1])


def _mod_kernel(c_ref, w_ref, b_ref, o_ref):
    sc = _silu(c_ref[...]).astype(BF16)
    o_ref[0] = _dot(sc, w_ref[0].astype(BF16)) + b_ref[0]


def _modulation(c_rows, w_mod, b_mod):
    depth, d, n = w_mod.shape
    rows = c_rows.shape[0]
    tn = d
    return pl.pallas_call(
        _mod_kernel,
        out_shape=jax.ShapeDtypeStruct((depth, rows, n), F32),
        grid=(depth, n // tn),
        in_specs=[pl.BlockSpec((rows, d), lambda l, j: (0, 0)),
                  pl.BlockSpec((1, d, tn), lambda l, j: (l, 0, j)),
                  pl.BlockSpec((1, 1, tn), lambda l, j: (l, 0, j))],
        out_specs=pl.BlockSpec((1, rows, tn), lambda l, j: (l, 0, j)),
        compiler_params=_cparams("parallel", "parallel"),
        name="modulation",
    )(c_rows, w_mod, b_mod.reshape(depth, 1, n))


def _rope_group(t, cos, sin_signed, even):
    partner = jnp.where(even, pltpu.roll(t, LANES - 1, axis=1), pltpu.roll(t, 1, axis=1))
    return t * cos + partner * sin_signed


def _head_rms(t, gain, bd):
    ss = _dot_exact_lhs(t * t, bd)
    return t * lax.rsqrt(ss * (1.0 / HEAD_DIM) + EPS) * gain


def _forget_features(z, lb):
    e = jnp.exp(-jnp.abs(z))
    inv = 1.0 / (1.0 + e)
    small = e * inv
    pos = z >= 0.0
    sig = jnp.where(pos, inv, small)
    sig_neg = jnp.where(pos, small, inv)
    return (1.0 - lb) * sig_neg, jnp.log(lb + (1.0 - lb) * sig) * LOG2_E


def _proj_kernel(x_ref, mlat_ref, mctx_ref, g_ref, w_ref, cos_ref, sin_ref, qn_ref, kn_ref,
                 lbf_ref, lbb_ref, bd_ref,
                 qa_ref, qb_ref, kv_ref, hq_ref, hk_ref, hlf_ref, hv_ref, hog_ref, gt_ref,
                 *, ctx_len, tm, d_model):
    pick = _mod_picker(mlat_ref, mctx_ref, pl.program_id(1) * tm, tm, ctx_len)
    h = _adaln(x_ref[0], g_ref[...], pick(0), pick(1)).astype(BF16)
    cos = cos_ref[...]
    sin = sin_ref[...]
    bd = bd_ref[...]
    lane = lax.broadcasted_iota(jnp.int32, (tm, LANES), 1)
    even = (lane & 1) == 0
    scale = HEAD_DIM ** -0.5
    groups = Q_WIDTH // LANES
    off_akv = Q_WIDTH
    off_bq = off_akv + 2 * KV_WIDTH
    off_bkv = off_bq + Q_WIDTH
    off_c = off_bkv + 2 * KV_WIDTH

    def cols(lo, width):
        return _dot(h, w_ref[:, lo:lo + width])

    def lanes(t, gidx):
        return t[:, gidx * LANES:(gidx + 1) * LANES]

    def gate(n):
        lo = FEAT_WIDTH + n * d_model
        gt_ref[0, :, n * d_model:(n + 1) * d_model] = _sigmoid(
            _dot(h, w_ref[:, lo:lo + d_model])).astype(BF16)

    kf, lff = _forget_features(cols(off_c + C_WIDTH, C_WIDTH), lbf_ref[...])
    hk_ref[0, :, 0:C_WIDTH] = kf
    hlf_ref[0, :, 0:C_WIDTH] = lff
    gate(0)
    kb, lfb = _forget_features(cols(off_c + 2 * C_WIDTH, C_WIDTH), lbb_ref[...])
    hk_ref[0, :, C_WIDTH:2 * C_WIDTH] = kb
    hlf_ref[0, :, C_WIDTH:2 * C_WIDTH] = lfb
    gate(1)
    bq = cols(off_bq, Q_WIDTH)
    qb_ref[0] = jnp.concatenate(
        [_rope_group(_head_rms(lanes(bq, gi), qn_ref[...], bd), cos, sin, even) * scale
         for gi in range(groups)], axis=1).astype(BF16)
    hv_ref[0] = cols(off_c + 3 * C_WIDTH, C_WIDTH)
    akv = cols(off_akv, 2 * KV_WIDTH)
    bkv = cols(off_bkv, 2 * KV_WIDTH)
    kv_ref[0] = jnp.concatenate(
        [_rope_group(lanes(akv, 0), cos, sin, even),
         _rope_group(_head_rms(lanes(bkv, 0), kn_ref[...], bd), cos, sin, even),
         lanes(akv, 1), lanes(bkv, 1)], axis=1).astype(BF16)
    gate(2)
    aq = cols(0, Q_WIDTH)
    qa_ref[0] = jnp.concatenate(
        [_rope_group(lanes(aq, gi), cos, sin, even) * scale for gi in range(groups)],
        axis=1).astype(BF16)
    hq_ref[0] = _silu(cols(off_c, C_WIDTH))
    hog_ref[0] = _silu(cols(off_c + 4 * C_WIDTH, C_WIDTH))


def _proj(x_all, m_lat, m_ctx, g, w, cos, sin, qn, kn, lbf, lbb, bd, *, ctx_len, tm):
    bsz, t, d = x_all.shape
    p = w.shape[1]
    row = lambda width: pl.BlockSpec((1, tm, width), lambda b, j: (b, j, 0))
    vec = lambda width: pl.BlockSpec((1, width), lambda b, j: (0, 0))
    outs = [(Q_WIDTH, BF16), (Q_WIDTH, BF16), (4 * KV_WIDTH, BF16), (C_WIDTH, F32),
            (2 * C_WIDTH, F32), (2 * C_WIDTH, F32), (C_WIDTH, F32), (C_WIDTH, F32),
            (N_BRANCH * d, BF16)]
    return pl.pallas_call(
        functools.partial(_proj_kernel, ctx_len=ctx_len, tm=tm, d_model=d),
        out_shape=[jax.ShapeDtypeStruct((bsz, t, wd), dt) for wd, dt in outs],
        grid=(bsz, t // tm),
        in_specs=[row(d),
                  pl.BlockSpec((1, N_MOD, d), lambda b, j: (b, 0, 0)),
                  pl.BlockSpec((N_MOD, d), lambda b, j: (0, 0)),
                  vec(d),
                  _resident((d, p)),
                  pl.BlockSpec((tm, LANES), lambda b, j: (j, 0)),
                  pl.BlockSpec((tm, LANES), lambda b, j: (j, 0)),
                  vec(LANES), vec(LANES), vec(C_WIDTH), vec(C_WIDTH),
                  pl.BlockSpec((3 * LANES, LANES), lambda b, j: (0, 0))],
        out_specs=[row(wd) for wd, _ in outs],
        compiler_params=_cparams("parallel", "parallel"),
        name="proj",
    )(x_all, m_lat, m_ctx, g, w, cos, sin, qn, kn, lbf, lbb, bd)


def _scan_constants(chunk):
    idx = np.arange(chunk)
    mats = []
    for direction in (0, 1):
        tri = (idx[None, :] <= idx[:, None]) if direction == 0 else (idx[None, :] >= idx[:, None])
        tri = tri.astype(np.float32)
        blocks = [tri, 1.0 - tri]
        h = chunk // 2
        while h >= 1:
            start = (idx // (2 * h)) * (2 * h)
            ref = start + h - 1 if direction == 0 else start + h
            is_q = ((idx // h) % 2) != direction
            diff = tri - tri[ref]
            blocks.append(np.where(is_q[:, None], diff, -diff))
            h //= 2
        mats.append(np.tile(np.concatenate(blocks, axis=0), (1, 3)))
    return np.stack(mats)


def _scan_kernel(dmat_ref, qf_ref, kf_ref, lff_ref, vf_ref, qb_ref, kb_ref, lfb_ref, vb_ref,
                 of_ref, ob_ref, st_ref, *, chunk, n_sub):
    @pl.when(pl.program_id(1) == 0)
    def _():
        st_ref[...] = jnp.zeros_like(st_ref)

    n_lev = chunk.bit_length() - 1
    dirs = (0, 1)
    rows = lambda sub: slice(sub * chunk, (sub + 1) * chunk)
    cols = lambda hh: slice(hh * C_DK, (hh + 1) * C_DK)
    q_all = (qf_ref[0], qb_ref[0])
    k_all = (kf_ref[0], kb_ref[0])
    v_all = (vf_ref[0].astype(BF16), vb_ref[0].astype(BF16))
    lf = (lff_ref[0], lfb_ref[0])
    e_all = {(d, sub): jnp.exp2(_dot_exact_rhs(dmat_ref[d], lf[d][rows(sub)]))
             for sub in range(n_sub) for d in dirs}

    row = lax.broadcasted_iota(jnp.int32, (chunk, 1), 0)
    r2 = lax.broadcasted_iota(jnp.int32, (chunk, chunk), 0)
    c2 = lax.broadcasted_iota(jnp.int32, (chunk, chunk), 1)
    heads = [(d, hh) for hh in range(C_HEADS) for d in dirs]

    scores = {}
    for sub in range(n_sub):
        for d, hh in heads:
            sl = cols(hh)
            q, k = q_all[d][rows(sub), sl], k_all[d][rows(sub), sl]
            later = (r2 > c2) if d == 0 else (r2 < c2)
            sc = jnp.where(r2 == c2, _dot_nt(q.astype(BF16), k.astype(BF16)), 0.0)
            for lev in range(n_lev):
                shift = n_lev - 1 - lev
                is_q = ((row >> shift) & 1) != d
                m = (jnp.where(is_q, q, k)
                     * e_all[d, sub][(2 + lev) * chunk:(3 + lev) * chunk, sl]).astype(BF16)
                pair = (later & ((r2 >> (shift + 1)) == (c2 >> (shift + 1)))
                        & ((r2 >> shift) != (c2 >> shift)))
                sc = jnp.where(pair, _dot_nt(m, m), sc)
            scores[d, sub, hh] = sc.astype(BF16)

    inter = {}
    for d, hh in heads:
        sl = cols(hh)
        st = st_ref[d, hh]
        for sub in (range(n_sub) if d == 0 else reversed(range(n_sub))):
            e_a = e_all[d, sub][0:chunk, sl]
            e_b = e_all[d, sub][chunk:2 * chunk, sl]
            inter[d, sub, hh] = _dot_nt((q_all[d][rows(sub), sl] * e_a).astype(BF16),
                                        st.astype(BF16))
            kd = (k_all[d][rows(sub), sl] * e_b).astype(BF16)
            st = st * (e_a[0:1] * e_b[0:1]) + _dot_tn(v_all[d][rows(sub), sl], kd)
        st_ref[d, hh] = st

    for d, o_ref in ((0, of_ref), (1, ob_ref)):
        for sub in range(n_sub):
            o_ref[0, rows(sub)] = jnp.concatenate(
                [inter[d, sub, hh] + _dot(scores[d, sub, hh], v_all[d][rows(sub), cols(hh)])
                 for hh in range(C_HEADS)], axis=1)


def _scan(dmats, hq, hk, hlf, hv, *, ctx_len, chunk, n_sub):
    bsz, t, _ = hq.shape
    blk = chunk * n_sub
    nc = t // blk
    nctx = ctx_len // blk

    def back(c):
        return jnp.where(c < nctx, nctx - 1 - c, nc - 1 + nctx - c)

    fwd = lambda col: pl.BlockSpec((1, blk, C_WIDTH), lambda b, c: (b, c, col))
    bwd = lambda col: pl.BlockSpec((1, blk, C_WIDTH), lambda b, c: (b, back(c), col))
    return pl.pallas_call(
        functools.partial(_scan_kernel, chunk=chunk, n_sub=n_sub),
        out_shape=[jax.ShapeDtypeStruct((bsz, t, C_WIDTH), F32)] * 2,
        grid=(bsz, nc),
        in_specs=[pl.BlockSpec(dmats.shape, lambda b, c: (0, 0, 0)),
                  fwd(0), fwd(0), fwd(0), fwd(0), bwd(0), bwd(1), bwd(1), bwd(0)],
        out_specs=[fwd(0), bwd(0)],
        scratch_shapes=[pltpu.VMEM((2, C_HEADS, C_DV, C_DK), F32)],
        compiler_params=_cparams("parallel", "arbitrary"),
        name="hgrn_scan",
    )(dmats, hq, hk, hlf, hv, hq, hk, hlf, hv)


def _stack_heads(q, kvh, group):
    return jnp.concatenate(
        [q[:, (kvh * group + g) * HEAD_DIM:(kvh * group + g + 1) * HEAD_DIM] for g in range(group)],
        axis=0)


def _unstack_heads(accs, group):
    pieces = []
    for kvh, acc in enumerate(accs):
        for g in range(group):
            blk = acc[:, g * BLOCK:(g + 1) * BLOCK].T
            o = blk * (1.0 / pltpu.roll(blk, HEAD_DIM, axis=1))
            pieces.append(o[:, kvh * HEAD_DIM:(kvh + 1) * HEAD_DIM])
    return jnp.concatenate(pieces, axis=1)


def _attn_a_kernel(sink_ref, q_ref, kv_ref, o_ref, *, ctx_len, t_len):
    i = pl.program_id(1)
    n_ctx_blocks = ctx_len // BLOCK
    group = A_HEADS // A_KV_HEADS
    band = 3 * BLOCK
    q = q_ref[0]

    def sink_col(kvh):
        r = lax.broadcasted_iota(jnp.int32, (group * BLOCK, 1), 0)
        col = jnp.full((group * BLOCK, 1), sink_ref[kvh * group], F32)
        for g in range(1, group):
            col = jnp.where(r >= g * BLOCK, sink_ref[kvh * group + g], col)
        return col

    def ctx_kv(kvh):
        kc = kv_ref[0, 0:ctx_len, kvh * HEAD_DIM:(kvh + 1) * HEAD_DIM]
        vc = kv_ref[0, 0:ctx_len, 2 * KV_WIDTH + kvh * HEAD_DIM:2 * KV_WIDTH + (kvh + 1) * HEAD_DIM]
        return kc, vc

    def finish(parts):
        o_ref[0] = jnp.concatenate(
            [o[g * BLOCK:(g + 1) * BLOCK] for o in parts for g in range(group)], axis=1).astype(BF16)

    @pl.when(i < n_ctx_blocks)
    def _():
        parts = []
        for kvh in range(A_KV_HEADS):
            q4 = _stack_heads(q, kvh, group)
            kc, vc = ctx_kv(kvh)
            sk = sink_col(kvh)
            s = _dot_nt(q4, kc)
            m = jnp.maximum(jnp.max(s, axis=-1, keepdims=True), sk)
            p = jnp.exp(s - m)
            den = jnp.sum(p, axis=-1, keepdims=True) + jnp.exp(sk - m)
            parts.append(_dot(p.astype(BF16), vc) * (1.0 / den))
        finish(parts)

    @pl.when(i >= n_ctx_blocks)
    def _():
        n = i - n_ctx_blocks
        start = pl.multiple_of(jnp.minimum(ctx_len + (n - 1) * BLOCK, t_len - band), BLOCK)
        rq = lax.broadcasted_iota(jnp.int32, (group * BLOCK, band), 0)
        ck = lax.broadcasted_iota(jnp.int32, (group * BLOCK, band), 1)
        qpos = n * BLOCK + (rq & (BLOCK - 1))
        kpos = start - ctx_len + ck
        valid = (jnp.abs(kpos - qpos) <= WINDOW) & (kpos >= 0)
        parts = []
        for kvh in range(A_KV_HEADS):
            q4 = _stack_heads(q, kvh, group)
            kc, vc = ctx_kv(kvh)
            kl = kv_ref[0, pl.ds(start, band), kvh * HEAD_DIM:(kvh + 1) * HEAD_DIM]
            vl = kv_ref[0, pl.ds(start, band),
                        2 * KV_WIDTH + kvh * HEAD_DIM:2 * KV_WIDTH + (kvh + 1) * HEAD_DIM]
            sk = sink_col(kvh)
            s_ctx = _dot_nt(q4, kc)
            s_loc = jnp.where(valid, _dot_nt(q4, kl), NEG_BIG)
            m = jnp.maximum(jnp.maximum(jnp.max(s_ctx, axis=-1, keepdims=True),
                                        jnp.max(s_loc, axis=-1, keepdims=True)), sk)
            p_ctx = jnp.exp(s_ctx - m)
            p_loc = jnp.exp(s_loc - m)
            den = (jnp.sum(p_ctx, axis=-1, keepdims=True) + jnp.sum(p_loc, axis=-1, keepdims=True)
                   + jnp.exp(sk - m))
            o = _dot(p_ctx.astype(BF16), vc) + _dot(p_loc.astype(BF16), vl)
            parts.append(o * (1.0 / den))
        finish(parts)


def _attn_a(sink, qa, kv, *, ctx_len):
    bsz, t, _ = qa.shape
    return pl.pallas_call(
        functools.partial(_attn_a_kernel, ctx_len=ctx_len, t_len=t),
        out_shape=jax.ShapeDtypeStruct((bsz, t, Q_WIDTH), BF16),
        grid=(bsz, t // BLOCK),
        in_specs=[pl.BlockSpec(memory_space=pltpu.SMEM),
                  pl.BlockSpec((1, BLOCK, Q_WIDTH), lambda b, i: (b, i, 0)),
                  pl.BlockSpec((1, t, 4 * KV_WIDTH), lambda b, i: (b, 0, 0))],
        out_specs=pl.BlockSpec((1, BLOCK, Q_WIDTH), lambda b, i: (b, i, 0)),
        compiler_params=_cparams("parallel", "arbitrary"),
        name="attn_window",
    )(sink, qa, kv)


def _attn_b_kernel(q_ref, kv_ref, o_ref, vt_ref, *, ctx_len, t_len, kc):
    i = pl.program_id(1)
    group = B_HEADS // B_KV_HEADS
    row = lax.broadcasted_iota(jnp.int32, (LANES, 1), 0)

    @pl.when(i == 0)
    def _():
        vt_ref[...] = kv_ref[0, :, 3 * KV_WIDTH:4 * KV_WIDTH].astype(F32).T.astype(BF16)

    def run(n_keys):
        q = q_ref[0]
        heads = range(B_KV_HEADS)
        q4 = [_stack_heads(q, kvh, group) for kvh in heads]
        own = [(row >= kvh * HEAD_DIM) & (row < (kvh + 1) * HEAD_DIM) for kvh in heads]
        bounds = [(k0, min(k0 + kc, n_keys)) for k0 in range(0, n_keys, kc)]

        def scores(c, kvh):
            k0, k1 = bounds[c]
            k = kv_ref[0, k0:k1, KV_WIDTH + kvh * HEAD_DIM:KV_WIDTH + (kvh + 1) * HEAD_DIM]
            return _dot_nt(k, q4[kvh])

        m = [jnp.full((1, group * BLOCK), NEG_BIG, F32) for _ in heads]
        acc = [jnp.zeros((LANES, group * BLOCK), F32) for _ in heads]
        s = [scores(0, kvh) for kvh in heads]
        for c, (k0, k1) in enumerate(bounds):
            s_next = [scores(c + 1, kvh) for kvh in heads] if c + 1 < len(bounds) else None
            vt = vt_ref[:, k0:k1]
            for kvh in heads:
                m_new = jnp.maximum(m[kvh], jnp.max(s[kvh], axis=0, keepdims=True))
                p = jnp.exp(s[kvh] - m_new).astype(BF16)
                vt_ext = jnp.where(own[kvh], vt, jnp.ones_like(vt))
                acc[kvh] = acc[kvh] * jnp.exp(m[kvh] - m_new) + _dot(vt_ext, p)
                m[kvh] = m_new
            s = s_next
        o_ref[0] = _unstack_heads(acc, group).astype(BF16)

    @pl.when(i < ctx_len // BLOCK)
    def _():
        run(ctx_len)

    @pl.when(i >= ctx_len // BLOCK)
    def _():
        run(t_len)


def _attn_b(qb, kv, *, ctx_len):
    bsz, t, _ = qb.shape
    return pl.pallas_call(
        functools.partial(_attn_b_kernel, ctx_len=ctx_len, t_len=t, kc=9 * BLOCK),
        out_shape=jax.ShapeDtypeStruct((bsz, t, Q_WIDTH), BF16),
        grid=(bsz, t // BLOCK),
        in_specs=[pl.BlockSpec((1, BLOCK, Q_WIDTH), lambda b, i: (b, i, 0)),
                  pl.BlockSpec((1, t, 4 * KV_WIDTH), lambda b, i: (b, 0, 0))],
        out_specs=pl.BlockSpec((1, BLOCK, Q_WIDTH), lambda b, i: (b, i, 0)),
        scratch_shapes=[pltpu.VMEM((LANES, t), BF16)],
        compiler_params=_cparams("parallel", "arbitrary"),
        name="attn_global",
    )(qb, kv)


def _merge_kernel(x_ref, ya_ref, yb_ref, of_ref, ob_ref, og_ref, gt_ref, gn_ref, wb_ref, wo_ref,
                  mlat_ref, mctx_ref, o_ref, *, ctx_len, tm, d_model):
    pick = _mod_picker(mlat_ref, mctx_ref, pl.program_id(1) * tm, tm, ctx_len)
    oc = of_ref[0] + ob_ref[0]
    og = og_ref[0]
    gn = gn_ref[...]
    heads = []
    for hh in range(C_HEADS):
        sl = slice(hh * C_DV, (hh + 1) * C_DV)
        t = oc[:, sl]
        y = t * lax.rsqrt(jnp.mean(t * t, axis=-1, keepdims=True) + EPS) * gn
        heads.append(y * og[:, sl])
    yc = jnp.concatenate(heads, axis=1).astype(BF16)
    merged = None
    for n, y in enumerate((ya_ref[0], yb_ref[0], yc)):
        up = _dot(y, wb_ref[n])
        term = gt_ref[0, :, n * d_model:(n + 1) * d_model].astype(F32) * up
        merged = term if merged is None else merged + term
    out = _dot(merged.astype(BF16), wo_ref[...])
    o_ref[0] = x_ref[0] + pick(2) * out


def _merge(x_all, ya, yb, o_fwd, o_bwd, hog, gates, gn, wb, wo, m_lat, m_ctx, *, ctx_len, tm):
    bsz, t, d = x_all.shape
    row = lambda width: pl.BlockSpec((1, tm, width), lambda b, j: (b, j, 0))
    return pl.pallas_call(
        functools.partial(_merge_kernel, ctx_len=ctx_len, tm=tm, d_model=d),
        out_shape=jax.ShapeDtypeStruct((bsz, t, d), F32),
        grid=(bsz, t // tm),
        in_specs=[row(d), row(Q_WIDTH), row(Q_WIDTH), row(C_WIDTH), row(C_WIDTH),
                  row(C_WIDTH), row(N_BRANCH * d),
                  pl.BlockSpec((1, C_DV), lambda b, j: (0, 0)),
                  _resident(wb.shape), _resident(wo.shape),
                  pl.BlockSpec((1, N_MOD, d), lambda b, j: (b, 0, 0)),
                  pl.BlockSpec((N_MOD, d), lambda b, j: (0, 0))],
        out_specs=row(d),
        compiler_params=_cparams("parallel", "parallel"),
        name="merge",
    )(x_all, ya, yb, o_fwd, o_bwd, hog, gates, gn, wb, wo, m_lat, m_ctx)


def _ffn_kernel(x_ref, xp_ref, xn_ref, mlat_ref, mctx_ref, g_ref, wu_ref, wdw_ref, bdw_ref, wd_ref,
                gf_ref, o_ref, *, ctx_len, t_len, tm, d_ff, fc, down_group, final):
    r0 = pl.program_id(1) * tm
    mod = _mod_rows(mlat_ref, mctx_ref, r0 < ctx_len)
    x = x_ref[0]
    norm = lambda rows_: _adaln(rows_, g_ref[...], mod[3:4], mod[4:5])
    keep_prev = jnp.where((r0 == 0) | (r0 == ctx_len), 0.0, 1.0)
    keep_next = jnp.where((r0 + tm == ctx_len) | (r0 + tm == t_len), 0.0, 1.0)
    h = jnp.concatenate([norm(xp_ref[0]) * keep_prev, norm(x), norm(xn_ref[0]) * keep_next],
                        axis=0).astype(BF16)
    rows = tm + 2 * SUBLANES
    mid = slice(SUBLANES, SUBLANES + tm)

    def up(ci):
        return (_dot(h, wu_ref[:, ci * fc:(ci + 1) * fc]),
                _dot(h, wu_ref[:, d_ff + ci * fc:d_ff + (ci + 1) * fc]))

    def conv(u, lo):
        w = wdw_ref[:, lo:lo + fc]
        return (w[0:1] * pltpu.roll(u, 1, axis=0)[mid] + w[1:2] * u[mid]
                + w[2:3] * pltpu.roll(u, rows - 1, axis=0)[mid] + bdw_ref[:, lo:lo + fc])

    n_chunks = d_ff // fc
    ups = [up(0)]
    acts, acc, k0 = [], None, 0
    for ci in range(n_chunks):
        if ci + 1 < n_chunks:
            ups.append(up(ci + 1))
        ua, ug = ups[ci]
        acts.append((conv(ua, ci * fc) * _silu(conv(ug, d_ff + ci * fc))).astype(BF16))
        if len(acts) == down_group or ci + 1 == n_chunks:
            k1 = (ci + 1) * fc
            part = _dot(jnp.concatenate(acts, axis=1), wd_ref[k0:k1, :])
            acc = part if acc is None else acc + part
            acts, k0 = [], k1
    y = x + mod[5:6] * acc
    if final:
        y = y * lax.rsqrt(jnp.mean(y * y, axis=-1, keepdims=True) + EPS) * gf_ref[...]
    o_ref[0] = y


def _ffn(x_mix, m_lat, m_ctx, g, wu, wdw, bdw, wd, g_final, *, ctx_len, tm, fc, final):
    bsz, t, d = x_mix.shape
    d_ff = wd.shape[0]
    per = tm // SUBLANES
    last_blk = t // SUBLANES - 1
    skip = ctx_len // tm if final else 0
    return pl.pallas_call(
        functools.partial(_ffn_kernel, ctx_len=ctx_len, t_len=t, tm=tm, d_ff=d_ff, fc=fc,
                          down_group=4, final=final),
        out_shape=jax.ShapeDtypeStruct((bsz, t - skip * tm, d), F32),
        grid=(bsz, t // tm),
        in_specs=[pl.BlockSpec((1, tm, d), lambda b, j: (b, j, 0)),
                  pl.BlockSpec((1, SUBLANES, d), lambda b, j: (b, jnp.maximum(j * per - 1, 0), 0)),
                  pl.BlockSpec((1, SUBLANES, d),
                               lambda b, j: (b, jnp.minimum((j + 1) * per, last_blk), 0)),
                  pl.BlockSpec((1, N_MOD, d), lambda b, j: (b, 0, 0)),
                  pl.BlockSpec((N_MOD, d), lambda b, j: (0, 0)),
                  pl.BlockSpec((1, d), lambda b, j: (0, 0)),
                  _resident(wu.shape),
                  pl.BlockSpec(wdw.shape, lambda b, j: (0, 0)),
                  pl.BlockSpec(bdw.shape, lambda b, j: (0, 0)),
                  _resident(wd.shape),
                  pl.BlockSpec((1, d), lambda b, j: (0, 0))],
        out_specs=pl.BlockSpec((1, tm, d), lambda b, j: (b, jnp.maximum(j - skip, 0), 0)),
        compiler_params=_cparams("parallel", "arbitrary"),
        name="conv_ffn",
    )(x_mix, x_mix, x_mix, m_lat, m_ctx, g, wu, wdw, bdw, wd, g_final)


def _rope_tables(ctx_len, seq):
    rows = seq // GRID_W
    row = jnp.repeat(jnp.arange(rows), GRID_W).astype(F32)
    col = jnp.tile(jnp.arange(GRID_W), rows).astype(F32)
    half = HEAD_DIM // 2
    inv = ROPE_THETA ** (-jnp.arange(0, half, 2, dtype=F32) / half)
    ang = jnp.concatenate([row[:, None] * inv, col[:, None] * inv], axis=-1)
    cos = jnp.concatenate([jnp.ones((ctx_len, half), F32), jnp.cos(ang)], axis=0)
    sin = jnp.concatenate([jnp.zeros((ctx_len, half), F32), jnp.sin(ang)], axis=0)
    reps = LANES // HEAD_DIM
    pairs = lambda a, b: jnp.stack([a, b], axis=-1).reshape(a.shape[0], HEAD_DIM)
    return jnp.tile(pairs(cos, cos), (1, reps)), jnp.tile(pairs(-sin, sin), (1, reps))


def _lower_bounds(raw):
    p = jax.nn.softmax(raw.astype(F32), axis=0)
    return jnp.clip(jnp.cumsum(p, axis=0) - p[0], 0.0, 1.0)


def _row_tile(ctx_len):
    tm = 256
    while ctx_len % tm:
        tm //= 2
    return tm


def _wide_tile(t, cap):
    return max(tm for tm in range(BLOCK, cap + 1, BLOCK) if t % tm == 0)


def _ffn_chunk(d_ff):
    for fc in (256, 128):
        if d_ff % fc == 0:
            return fc
    return d_ff


def kernel(x, c, ctx, c_ctx, w_mod, b_mod, norm_mix, norm_ffn, w_in, sink_a, qn_b, kn_b, lb_fwd,
           lb_bwd, gn_c, w_branch, w_out, w_up, w_dw, b_dw, w_down, norm_final):
    bsz, seq, d = x.shape
    ctx_len = ctx.shape[1]
    depth = w_mod.shape[0]
    assert ctx_len % BLOCK == 0 and seq % BLOCK == 0 and ctx_len >= BLOCK and seq >= 3 * BLOCK
    assert d % LANES == 0
    tm = _row_tile(ctx_len)
    tm_proj = _wide_tile(ctx_len + seq, 384)
    tm_merge = _wide_tile(ctx_len + seq, 768)
    chunk = 64
    n_sub = max(n for n in (1, 2, 4) if ctx_len % (n * chunk) == 0 and seq % (n * chunk) == 0)
    fc = _ffn_chunk(w_down.shape[1])

    n_rows = -(-(bsz + 1) // SUBLANES) * SUBLANES
    c_rows = jnp.concatenate([c, c_ctx[None], jnp.zeros((n_rows - bsz - 1, d), F32)], axis=0)
    mods = _modulation(c_rows, w_mod, b_mod)
    m_lat = mods[:, :bsz].reshape(depth, bsz, N_MOD, d)
    m_ctx = mods[:, bsz].reshape(depth, N_MOD, d)

    w_in_p = w_in.astype(BF16)
    wb_bf = w_branch.astype(BF16)
    wo_bf = w_out.astype(BF16)
    wu_bf = w_up.astype(BF16)
    wd_bf = w_down.astype(BF16)
    qn_p = jnp.tile(qn_b, (1, LANES // HEAD_DIM))
    kn_p = jnp.tile(kn_b, (1, LANES // HEAD_DIM))
    lbf = _lower_bounds(lb_fwd)
    lbb = _lower_bounds(lb_bwd)
    cos, sin = _rope_tables(ctx_len, seq)
    lane = np.arange(LANES)
    bd = jnp.asarray(np.tile(lane[:, None] // HEAD_DIM == lane[None, :] // HEAD_DIM, (3, 1)), BF16)
    dmats = jnp.asarray(_scan_constants(chunk), BF16)

    x_all = jnp.concatenate([ctx, x], axis=1)
    for l in range(depth):
        qa, qb, kv, hq, hk, hlf, hv, hog, gates = _proj(
            x_all, m_lat[l], m_ctx[l], norm_mix[l][None], w_in_p[l], cos, sin,
            qn_p[l][None], kn_p[l][None], lbf[l][None], lbb[l][None], bd, ctx_len=ctx_len,
            tm=tm_proj)
        o_fwd, o_bwd = _scan(dmats, hq, hk, hlf, hv, ctx_len=ctx_len, chunk=chunk, n_sub=n_sub)
        ya = _attn_a(sink_a[l], qa, kv, ctx_len=ctx_len)
        yb = _attn_b(qb, kv, ctx_len=ctx_len)
        x_mix = _merge(x_all, ya, yb, o_fwd, o_bwd, hog, gates, gn_c[l][None], wb_bf[l], wo_bf[l],
                       m_lat[l], m_ctx[l], ctx_len=ctx_len, tm=tm_merge)
        x_all = _ffn(x_mix, m_lat[l], m_ctx[l], norm_ffn[l][None], wu_bf[l], w_dw[l],
                     b_dw[l][None], wd_bf[l], norm_final[None], ctx_len=ctx_len, tm=tm, fc=fc,
                     final=l == depth - 1)
    return x_all
```

```python
import functools

import numpy as np
import jax
import jax.numpy as jnp
from jax import lax
from jax.experimental import pallas as pl
from jax.experimental.pallas import tpu as pltpu

HEAD_DIM = 64
A_HEADS = 8
A_KV_HEADS = 2
B_HEADS = 8
B_KV_HEADS = 2
C_HEADS = 4
C_DK = 128
C_DV = 128
N_BRANCH = 3
WINDOW = 128
BLOCK = 128
GRID_W = 64
ROPE_THETA = 10000.0
N_MOD = 6
EPS = 1e-6
LOG2_E = 1.4426950408889634
NEG_BIG = -1e30
Q_WIDTH = A_HEADS * HEAD_DIM
KV_WIDTH = A_KV_HEADS * HEAD_DIM
C_WIDTH = C_HEADS * C_DK
FEAT_WIDTH = 2 * Q_WIDTH + 4 * KV_WIDTH + 5 * C_WIDTH
LANES = 128
SUBLANES = 8
VMEM_LIMIT = 56 * 1024 * 1024

F32 = jnp.float32
BF16 = jnp.bfloat16


def _cparams(*sem):
    return pltpu.CompilerParams(dimension_semantics=sem, vmem_limit_bytes=VMEM_LIMIT)


def _resident(shape):
    zeros = (0,) * len(shape)
    return pl.BlockSpec(shape, lambda *_: zeros, pipeline_mode=pl.Buffered(1))


def _dot(a, b):
    return jnp.dot(a, b, preferred_element_type=F32)


def _dot_nt(a, b):
    return lax.dot_general(a, b, (((1,), (1,)), ((), ())), preferred_element_type=F32)


def _dot_tn(a, b):
    return lax.dot_general(a, b, (((0,), (0,)), ((), ())), preferred_element_type=F32)


def _split3(x):
    hi = x.astype(BF16)
    r1 = x - hi.astype(F32)
    mid = r1.astype(BF16)
    lo = (r1 - mid.astype(F32)).astype(BF16)
    return hi, mid, lo


def _dot_exact_rhs(m3_bf16, x):
    return _dot(m3_bf16, jnp.concatenate(_split3(x), axis=0))


def _dot_exact_lhs(x, m3_bf16):
    return _dot(jnp.concatenate(_split3(x), axis=1), m3_bf16)


def _silu(x):
    return x * (1.0 / (1.0 + jnp.exp(-x)))


def _sigmoid(x):
    return 1.0 / (1.0 + jnp.exp(-x))


def _adaln(x, g, shift, scale):
    y = x * lax.rsqrt(jnp.mean(x * x, axis=-1, keepdims=True) + EPS)
    return (y * g) * (1.0 + scale) + shift


def _mod_rows(mlat_ref, mctx_ref, is_ctx):
    return jnp.where(is_ctx, mctx_ref[...], mlat_ref[0])


def _mod_picker(mlat_ref, mctx_ref, r0, tm, ctx_len):
    if ctx_len % tm == 0:
        mod = _mod_rows(mlat_ref, mctx_ref, r0 < ctx_len)
        return lambda i: mod[i:i + 1]
    is_ctx = r0 + lax.broadcasted_iota(jnp.int32, (tm, 1), 0) < ctx_len
    return lambda i: jnp.where(is_ctx, mctx_ref[i:i + 1], mlat_ref[0, i:i + 1])


def _mod_kernel(c_ref, w_ref, b_ref, o_ref):
    sc = _silu(c_ref[...]).astype(BF16)
    o_ref[0] = _dot(sc, w_ref[0].astype(BF16)) + b_ref[0]


def _modulation(c_rows, w_mod, b_mod):
    depth, d, n = w_mod.shape
    rows = c_rows.shape[0]
    tn = d
    return pl.pallas_call(
        _mod_kernel,
        out_shape=jax.ShapeDtypeStruct((depth, rows, n), F32),
        grid=(depth, n // tn),
        in_specs=[pl.BlockSpec((rows, d), lambda l, j: (0, 0)),
                  pl.BlockSpec((1, d, tn), lambda l, j: (l, 0, j)),
                  pl.BlockSpec((1, 1, tn), lambda l, j: (l, 0, j))],
        out_specs=pl.BlockSpec((1, rows, tn), lambda l, j: (l, 0, j)),
        compiler_params=_cparams("parallel", "parallel"),
        name="modulation",
    )(c_rows, w_mod, b_mod.reshape(depth, 1, n))


def _rope_group(t, cos, sin_signed, even):
    partner = jnp.where(even, pltpu.roll(t, LANES - 1, axis=1), pltpu.roll(t, 1, axis=1))
    return t * cos + partner * sin_signed


def _head_rms(t, gain, bd):
    ss = _dot_exact_lhs(t * t, bd)
    return t * lax.rsqrt(ss * (1.0 / HEAD_DIM) + EPS) * gain


def _forget_features(z, lb):
    e = jnp.exp(-jnp.abs(z))
    inv = 1.0 / (1.0 + e)
    small = e * inv
    pos = z >= 0.0
    sig = jnp.where(pos, inv, small)
    sig_neg = jnp.where(pos, small, inv)
    return (1.0 - lb) * sig_neg, jnp.log(lb + (1.0 - lb) * sig) * LOG2_E


def _proj_kernel(x_ref, mlat_ref, mctx_ref, g_ref, w_ref, cos_ref, sin_ref, qn_ref, kn_ref,
                 lbf_ref, lbb_ref, bd_ref,
                 qa_ref, qb_ref, kv_ref, hq_ref, hk_ref, hlf_ref, hv_ref, hog_ref, gt_ref,
                 *, ctx_len, tm, d_model):
    pick = _mod_picker(mlat_ref, mctx_ref, pl.program_id(1) * tm, tm, ctx_len)
    h = _adaln(x_ref[0], g_ref[...], pick(0), pick(1)).astype(BF16)
    cos = cos_ref[...]
    sin = sin_ref[...]
    bd = bd_ref[...]
    lane = lax.broadcasted_iota(jnp.int32, (tm, LANES), 1)
    even = (lane & 1) == 0
    scale = HEAD_DIM ** -0.5 * LOG2_E
    groups = Q_WIDTH // LANES
    off_akv = Q_WIDTH
    off_bq = off_akv + 2 * KV_WIDTH
    off_bkv = off_bq + Q_WIDTH
    off_c = off_bkv + 2 * KV_WIDTH

    def cols(lo, width):
        return _dot(h, w_ref[:, lo:lo + width])

    def lanes(t, gidx):
        return t[:, gidx * LANES:(gidx + 1) * LANES]

    def gate(n):
        lo = FEAT_WIDTH + n * d_model
        gt_ref[0, :, n * d_model:(n + 1) * d_model] = _sigmoid(
            _dot(h, w_ref[:, lo:lo + d_model])).astype(BF16)

    kf, lff = _forget_features(cols(off_c + C_WIDTH, C_WIDTH), lbf_ref[...])
    hk_ref[0, :, 0:C_WIDTH] = kf
    hlf_ref[0, :, 0:C_WIDTH] = lff
    gate(0)
    kb, lfb = _forget_features(cols(off_c + 2 * C_WIDTH, C_WIDTH), lbb_ref[...])
    hk_ref[0, :, C_WIDTH:2 * C_WIDTH] = kb
    hlf_ref[0, :, C_WIDTH:2 * C_WIDTH] = lfb
    gate(1)
    bq = cols(off_bq, Q_WIDTH)
    qb_ref[0] = jnp.concatenate(
        [_rope_group(_head_rms(lanes(bq, gi), qn_ref[...], bd), cos, sin, even) * scale
         for gi in range(groups)], axis=1).astype(BF16)
    hv_ref[0] = cols(off_c + 3 * C_WIDTH, C_WIDTH)
    akv = cols(off_akv, 2 * KV_WIDTH)
    bkv = cols(off_bkv, 2 * KV_WIDTH)
    kv_ref[0] = jnp.concatenate(
        [_rope_group(lanes(akv, 0), cos, sin, even),
         _rope_group(_head_rms(lanes(bkv, 0), kn_ref[...], bd), cos, sin, even),
         lanes(akv, 1), lanes(bkv, 1)], axis=1).astype(BF16)
    gate(2)
    aq = cols(0, Q_WIDTH)
    qa_ref[0] = jnp.concatenate(
        [_rope_group(lanes(aq, gi), cos, sin, even) * scale for gi in range(groups)],
        axis=1).astype(BF16)
    hq_ref[0] = _silu(cols(off_c, C_WIDTH))
    hog_ref[0] = _silu(cols(off_c + 4 * C_WIDTH, C_WIDTH))


def _proj(x_all, m_lat, m_ctx, g, w, cos, sin, qn, kn, lbf, lbb, bd, *, ctx_len, tm):
    bsz, t, d = x_all.shape
    p = w.shape[1]
    row = lambda width: pl.BlockSpec((1, tm, width), lambda b, j: (b, j, 0))
    vec = lambda width: pl.BlockSpec((1, width), lambda b, j: (0, 0))
    outs = [(Q_WIDTH, BF16), (Q_WIDTH, BF16), (4 * KV_WIDTH, BF16), (C_WIDTH, F32),
            (2 * C_WIDTH, F32), (2 * C_WIDTH, F32), (C_WIDTH, F32), (C_WIDTH, F32),
            (N_BRANCH * d, BF16)]
    return pl.pallas_call(
        functools.partial(_proj_kernel, ctx_len=ctx_len, tm=tm, d_model=d),
        out_shape=[jax.ShapeDtypeStruct((bsz, t, wd), dt) for wd, dt in outs],
        grid=(bsz, t // tm),
        in_specs=[row(d),
                  pl.BlockSpec((1, N_MOD, d), lambda b, j: (b, 0, 0)),
                  pl.BlockSpec((N_MOD, d), lambda b, j: (0, 0)),
                  vec(d),
                  _resident((d, p)),
                  pl.BlockSpec((tm, LANES), lambda b, j: (j, 0)),
                  pl.BlockSpec((tm, LANES), lambda b, j: (j, 0)),
                  vec(LANES), vec(LANES), vec(C_WIDTH), vec(C_WIDTH),
                  pl.BlockSpec((3 * LANES, LANES), lambda b, j: (0, 0))],
        out_specs=[row(wd) for wd, _ in outs],
        compiler_params=_cparams("parallel", "parallel"),
        name="proj",
    )(x_all, m_lat, m_ctx, g, w, cos, sin, qn, kn, lbf, lbb, bd)


def _scan_constants(chunk):
    idx = np.arange(chunk)
    mats = []
    for direction in (0, 1):
        tri = (idx[None, :] <= idx[:, None]) if direction == 0 else (idx[None, :] >= idx[:, None])
        tri = tri.astype(np.float32)
        blocks = [tri, 1.0 - tri]
        h = chunk // 2
        while h >= 1:
            start = (idx // (2 * h)) * (2 * h)
            ref = start + h - 1 if direction == 0 else start + h
            is_q = ((idx // h) % 2) != direction
            diff = tri - tri[ref]
            blocks.append(np.where(is_q[:, None], diff, -diff))
            h //= 2
        mats.append(np.tile(np.concatenate(blocks, axis=0), (1, 3)))
    return np.stack(mats)


def _scan_kernel(dmat_ref, qf_ref, kf_ref, lff_ref, vf_ref, qb_ref, kb_ref, lfb_ref, vb_ref,
                 of_ref, ob_ref, st_ref, *, chunk, n_sub):
    @pl.when(pl.program_id(1) == 0)
    def _():
        st_ref[...] = jnp.zeros_like(st_ref)

    n_lev = chunk.bit_length() - 1
    dirs = (0, 1)
    rows = lambda sub: slice(sub * chunk, (sub + 1) * chunk)
    cols = lambda hh: slice(hh * C_DK, (hh + 1) * C_DK)
    q_all = (qf_ref[0], qb_ref[0])
    k_all = (kf_ref[0], kb_ref[0])
    v_all = (vf_ref[0].astype(BF16), vb_ref[0].astype(BF16))
    lf = (lff_ref[0], lfb_ref[0])
    e_all = {(d, sub): jnp.exp2(_dot_exact_rhs(dmat_ref[d], lf[d][rows(sub)]))
             for sub in range(n_sub) for d in dirs}

    row = lax.broadcasted_iota(jnp.int32, (chunk, 1), 0)
    r2 = lax.broadcasted_iota(jnp.int32, (chunk, chunk), 0)
    c2 = lax.broadcasted_iota(jnp.int32, (chunk, chunk), 1)
    heads = [(d, hh) for hh in range(C_HEADS) for d in dirs]

    scores = {}
    for sub in range(n_sub):
        for d, hh in heads:
            sl = cols(hh)
            q, k = q_all[d][rows(sub), sl], k_all[d][rows(sub), sl]
            later = (r2 > c2) if d == 0 else (r2 < c2)
            sc = jnp.where(r2 == c2, _dot_nt(q.astype(BF16), k.astype(BF16)), 0.0)
            for lev in range(n_lev):
                shift = n_lev - 1 - lev
                is_q = ((row >> shift) & 1) != d
                m = (jnp.where(is_q, q, k)
                     * e_all[d, sub][(2 + lev) * chunk:(3 + lev) * chunk, sl]).astype(BF16)
                pair = (later & ((r2 >> (shift + 1)) == (c2 >> (shift + 1)))
                        & ((r2 >> shift) != (c2 >> shift)))
                sc = jnp.where(pair, _dot_nt(m, m), sc)
            scores[d, sub, hh] = sc.astype(BF16)

    inter = {}
    for d, hh in heads:
        sl = cols(hh)
        st = st_ref[d, hh]
        for sub in (range(n_sub) if d == 0 else reversed(range(n_sub))):
            e_a = e_all[d, sub][0:chunk, sl]
            e_b = e_all[d, sub][chunk:2 * chunk, sl]
            inter[d, sub, hh] = _dot_nt((q_all[d][rows(sub), sl] * e_a).astype(BF16),
                                        st.astype(BF16))
            kd = (k_all[d][rows(sub), sl] * e_b).astype(BF16)
            st = st * (e_a[0:1] * e_b[0:1]) + _dot_tn(v_all[d][rows(sub), sl], kd)
        st_ref[d, hh] = st

    for d, o_ref in ((0, of_ref), (1, ob_ref)):
        for sub in range(n_sub):
            o_ref[0, rows(sub)] = jnp.concatenate(
                [inter[d, sub, hh] + _dot(scores[d, sub, hh], v_all[d][rows(sub), cols(hh)])
                 for hh in range(C_HEADS)], axis=1)


def _scan(dmats, hq, hk, hlf, hv, *, ctx_len, chunk, n_sub):
    bsz, t, _ = hq.shape
    blk = chunk * n_sub
    nc = t // blk
    nctx = ctx_len // blk

    def back(c):
        return jnp.where(c < nctx, nctx - 1 - c, nc - 1 + nctx - c)

    fwd = lambda col: pl.BlockSpec((1, blk, C_WIDTH), lambda b, c: (b, c, col))
    bwd = lambda col: pl.BlockSpec((1, blk, C_WIDTH), lambda b, c: (b, back(c), col))
    return pl.pallas_call(
        functools.partial(_scan_kernel, chunk=chunk, n_sub=n_sub),
        out_shape=[jax.ShapeDtypeStruct((bsz, t, C_WIDTH), F32)] * 2,
        grid=(bsz, nc),
        in_specs=[pl.BlockSpec(dmats.shape, lambda b, c: (0, 0, 0)),
                  fwd(0), fwd(0), fwd(0), fwd(0), bwd(0), bwd(1), bwd(1), bwd(0)],
        out_specs=[fwd(0), bwd(0)],
        scratch_shapes=[pltpu.VMEM((2, C_HEADS, C_DV, C_DK), F32)],
        compiler_params=_cparams("parallel", "arbitrary"),
        name="hgrn_scan",
    )(dmats, hq, hk, hlf, hv, hq, hk, hlf, hv)


def _stack_heads(q, kvh, group):
    return jnp.concatenate(
        [q[:, (kvh * group + g) * HEAD_DIM:(kvh * group + g + 1) * HEAD_DIM] for g in range(group)],
        axis=0)


def _unstack_heads(accs, group):
    pieces = []
    for kvh, acc in enumerate(accs):
        for g in range(group):
            blk = acc[:, g * BLOCK:(g + 1) * BLOCK].T
            o = blk * (1.0 / pltpu.roll(blk, HEAD_DIM, axis=1))
            pieces.append(o[:, kvh * HEAD_DIM:(kvh + 1) * HEAD_DIM])
    return jnp.concatenate(pieces, axis=1)


def _attn_a_kernel(sink_ref, q_ref, kv_ref, o_ref, *, ctx_len, t_len):
    i = pl.program_id(1)
    n_ctx_blocks = ctx_len // BLOCK
    group = A_HEADS // A_KV_HEADS
    band = 3 * BLOCK
    q = q_ref[0]

    def sink_col(kvh):
        r = lax.broadcasted_iota(jnp.int32, (group * BLOCK, 1), 0)
        col = jnp.full((group * BLOCK, 1), sink_ref[kvh * group], F32)
        for g in range(1, group):
            col = jnp.where(r >= g * BLOCK, sink_ref[kvh * group + g], col)
        return col * LOG2_E

    def ctx_kv(kvh):
        kc = kv_ref[0, 0:ctx_len, kvh * HEAD_DIM:(kvh + 1) * HEAD_DIM]
        vc = kv_ref[0, 0:ctx_len, 2 * KV_WIDTH + kvh * HEAD_DIM:2 * KV_WIDTH + (kvh + 1) * HEAD_DIM]
        return kc, vc

    def finish(parts):
        o_ref[0] = jnp.concatenate(
            [o[g * BLOCK:(g + 1) * BLOCK] for o in parts for g in range(group)], axis=1).astype(BF16)

    @pl.when(i < n_ctx_blocks)
    def _():
        parts = []
        for kvh in range(A_KV_HEADS):
            q4 = _stack_heads(q, kvh, group)
            kc, vc = ctx_kv(kvh)
            sk = sink_col(kvh)
            s = _dot_nt(q4, kc)
            m = jnp.maximum(jnp.max(s, axis=-1, keepdims=True), sk)
            p = jnp.exp2(s - m)
            den = jnp.sum(p, axis=-1, keepdims=True) + jnp.exp2(sk - m)
            parts.append(_dot(p.astype(BF16), vc) * (1.0 / den))
        finish(parts)

    @pl.when(i >= n_ctx_blocks)
    def _():
        n = i - n_ctx_blocks
        start = pl.multiple_of(jnp.minimum(ctx_len + (n - 1) * BLOCK, t_len - band), BLOCK)
        rq = lax.broadcasted_iota(jnp.int32, (group * BLOCK, band), 0)
        ck = lax.broadcasted_iota(jnp.int32, (group * BLOCK, band), 1)
        qpos = n * BLOCK + (rq & (BLOCK - 1))
        kpos = start - ctx_len + ck
        valid = (jnp.abs(kpos - qpos) <= WINDOW) & (kpos >= 0)
        parts = []
        for kvh in range(A_KV_HEADS):
            q4 = _stack_heads(q, kvh, group)
            kc, vc = ctx_kv(kvh)
            kl = kv_ref[0, pl.ds(start, band), kvh * HEAD_DIM:(kvh + 1) * HEAD_DIM]
            vl = kv_ref[0, pl.ds(start, band),
                        2 * KV_WIDTH + kvh * HEAD_DIM:2 * KV_WIDTH + (kvh + 1) * HEAD_DIM]
            sk = sink_col(kvh)
            s_ctx = _dot_nt(q4, kc)
            s_loc = jnp.where(valid, _dot_nt(q4, kl), NEG_BIG)
            m = jnp.maximum(jnp.maximum(jnp.max(s_ctx, axis=-1, keepdims=True),
                                        jnp.max(s_loc, axis=-1, keepdims=True)), sk)
            p_ctx = jnp.exp2(s_ctx - m)
            p_loc = jnp.exp2(s_loc - m)
            den = (jnp.sum(p_ctx, axis=-1, keepdims=True) + jnp.sum(p_loc, axis=-1, keepdims=True)
                   + jnp.exp2(sk - m))
            o = _dot(p_ctx.astype(BF16), vc) + _dot(p_loc.astype(BF16), vl)
            parts.append(o * (1.0 / den))
        finish(parts)


def _attn_a(sink, qa, kv, *, ctx_len):
    bsz, t, _ = qa.shape
    return pl.pallas_call(
        functools.partial(_attn_a_kernel, ctx_len=ctx_len, t_len=t),
        out_shape=jax.ShapeDtypeStruct((bsz, t, Q_WIDTH), BF16),
        grid=(bsz, t // BLOCK),
        in_specs=[pl.BlockSpec(memory_space=pltpu.SMEM),
                  pl.BlockSpec((1, BLOCK, Q_WIDTH), lambda b, i: (b, i, 0)),
                  pl.BlockSpec((1, t, 4 * KV_WIDTH), lambda b, i: (b, 0, 0))],
        out_specs=pl.BlockSpec((1, BLOCK, Q_WIDTH), lambda b, i: (b, i, 0)),
        compiler_params=_cparams("parallel", "arbitrary"),
        name="attn_window",
    )(sink, qa, kv)


def _attn_b_kernel(q_ref, kv_ref, o_ref, vt_ref, *, ctx_len, t_len, kc):
    i = pl.program_id(1)
    group = B_HEADS // B_KV_HEADS
    row = lax.broadcasted_iota(jnp.int32, (LANES, 1), 0)

    @pl.when(i == 0)
    def _():
        vt_ref[...] = kv_ref[0, :, 3 * KV_WIDTH:4 * KV_WIDTH].astype(F32).T.astype(BF16)

    def run(n_keys):
        q = q_ref[0]
        heads = range(B_KV_HEADS)
        q4 = [_stack_heads(q, kvh, group) for kvh in heads]
        own = [(row >= kvh * HEAD_DIM) & (row < (kvh + 1) * HEAD_DIM) for kvh in heads]
        bounds = [(k0, min(k0 + kc, n_keys)) for k0 in range(0, n_keys, kc)]

        def scores(c, kvh):
            k0, k1 = bounds[c]
            k = kv_ref[0, k0:k1, KV_WIDTH + kvh * HEAD_DIM:KV_WIDTH + (kvh + 1) * HEAD_DIM]
            return _dot_nt(k, q4[kvh])

        m = [jnp.full((1, group * BLOCK), NEG_BIG, F32) for _ in heads]
        acc = [jnp.zeros((LANES, group * BLOCK), F32) for _ in heads]
        s = [scores(0, kvh) for kvh in heads]
        for c, (k0, k1) in enumerate(bounds):
            s_next = [scores(c + 1, kvh) for kvh in heads] if c + 1 < len(bounds) else None
            vt = vt_ref[:, k0:k1]
            for kvh in heads:
                m_new = jnp.maximum(m[kvh], jnp.max(s[kvh], axis=0, keepdims=True))
                p = jnp.exp2(s[kvh] - m_new).astype(BF16)
                vt_ext = jnp.where(own[kvh], vt, jnp.ones_like(vt))
                acc[kvh] = acc[kvh] * jnp.exp2(m[kvh] - m_new) + _dot(vt_ext, p)
                m[kvh] = m_new
            s = s_next
        o_ref[0] = _unstack_heads(acc, group).astype(BF16)

    @pl.when(i < ctx_len // BLOCK)
    def _():
        run(ctx_len)

    @pl.when(i >= ctx_len // BLOCK)
    def _():
        run(t_len)


def _attn_b(qb, kv, *, ctx_len):
    bsz, t, _ = qb.shape
    return pl.pallas_call(
        functools.partial(_attn_b_kernel, ctx_len=ctx_len, t_len=t, kc=9 * BLOCK),
        out_shape=jax.ShapeDtypeStruct((bsz, t, Q_WIDTH), BF16),
        grid=(bsz, t // BLOCK),
        in_specs=[pl.BlockSpec((1, BLOCK, Q_WIDTH), lambda b, i: (b, i, 0)),
                  pl.BlockSpec((1, t, 4 * KV_WIDTH), lambda b, i: (b, 0, 0))],
        out_specs=pl.BlockSpec((1, BLOCK, Q_WIDTH), lambda b, i: (b, i, 0)),
        scratch_shapes=[pltpu.VMEM((LANES, t), BF16)],
        compiler_params=_cparams("parallel", "arbitrary"),
        name="attn_global",
    )(qb, kv)


def _merge_kernel(x_ref, ya_ref, yb_ref, of_ref, ob_ref, og_ref, gt_ref, gn_ref, wb_ref, wo_ref,
                  mlat_ref, mctx_ref, o_ref, *, ctx_len, tm, d_model):
    pick = _mod_picker(mlat_ref, mctx_ref, pl.program_id(1) * tm, tm, ctx_len)
    oc = of_ref[0] + ob_ref[0]
    og = og_ref[0]
    gn = gn_ref[...]
    heads = []
    for hh in range(C_HEADS):
        sl = slice(hh * C_DV, (hh + 1) * C_DV)
        t = oc[:, sl]
        y = t * lax.rsqrt(jnp.mean(t * t, axis=-1, keepdims=True) + EPS) * gn
        heads.append(y * og[:, sl])
    yc = jnp.concatenate(heads, axis=1).astype(BF16)
    merged = None
    for n, y in enumerate((ya_ref[0], yb_ref[0], yc)):
        up = _dot(y, wb_ref[n])
        term = gt_ref[0, :, n * d_model:(n + 1) * d_model].astype(F32) * up
        merged = term if merged is None else merged + term
    out = _dot(merged.astype(BF16), wo_ref[...])
    o_ref[0] = x_ref[0] + pick(2) * out


def _merge(x_all, ya, yb, o_fwd, o_bwd, hog, gates, gn, wb, wo, m_lat, m_ctx, *, ctx_len, tm):
    bsz, t, d = x_all.shape
    row = lambda width: pl.BlockSpec((1, tm, width), lambda b, j: (b, j, 0))
    return pl.pallas_call(
        functools.partial(_merge_kernel, ctx_len=ctx_len, tm=tm, d_model=d),
        out_shape=jax.ShapeDtypeStruct((bsz, t, d), F32),
        grid=(bsz, t // tm),
        in_specs=[row(d), row(Q_WIDTH), row(Q_WIDTH), row(C_WIDTH), row(C_WIDTH),
                  row(C_WIDTH), row(N_BRANCH * d),
                  pl.BlockSpec((1, C_DV), lambda b, j: (0, 0)),
                  _resident(wb.shape), _resident(wo.shape),
                  pl.BlockSpec((1, N_MOD, d), lambda b, j: (b, 0, 0)),
                  pl.BlockSpec((N_MOD, d), lambda b, j: (0, 0))],
        out_specs=row(d),
        compiler_params=_cparams("parallel", "parallel"),
        name="merge",
    )(x_all, ya, yb, o_fwd, o_bwd, hog, gates, gn, wb, wo, m_lat, m_ctx)


def _ffn_kernel(x_ref, xp_ref, xn_ref, mlat_ref, mctx_ref, g_ref, wu_ref, wdw_ref, bdw_ref, wd_ref,
                gf_ref, o_ref, *, ctx_len, t_len, tm, d_ff, fc, final, first_tile):
    r0 = (pl.program_id(1) + first_tile) * tm
    mod = _mod_rows(mlat_ref, mctx_ref, r0 < ctx_len)
    x = x_ref[0]
    norm = lambda rows_: _adaln(rows_, g_ref[...], mod[3:4], mod[4:5])
    keep_prev = jnp.where((r0 == 0) | (r0 == ctx_len), 0.0, 1.0)
    keep_next = jnp.where((r0 + tm == ctx_len) | (r0 + tm == t_len), 0.0, 1.0)
    h = jnp.concatenate([norm(xp_ref[0]) * keep_prev, norm(x), norm(xn_ref[0]) * keep_next],
                        axis=0).astype(BF16)
    rows = tm + 2 * SUBLANES
    mid = slice(SUBLANES, SUBLANES + tm)

    def up(ci):
        return (_dot(h, wu_ref[:, ci * fc:(ci + 1) * fc]),
                _dot(h, wu_ref[:, d_ff + ci * fc:d_ff + (ci + 1) * fc]))

    def conv(u, lo):
        w = wdw_ref[:, lo:lo + fc]
        return (w[0:1] * pltpu.roll(u, 1, axis=0)[mid] + w[1:2] * u[mid]
                + w[2:3] * pltpu.roll(u, rows - 1, axis=0)[mid] + bdw_ref[:, lo:lo + fc])

    n_chunks = d_ff // fc
    ups = [up(0)]
    acts = []
    for ci in range(n_chunks):
        if ci + 1 < n_chunks:
            ups.append(up(ci + 1))
        ua, ug = ups[ci]
        acts.append((conv(ua, ci * fc) * _silu(conv(ug, d_ff + ci * fc))).astype(BF16))
    y = x + mod[5:6] * _dot(jnp.concatenate(acts, axis=1), wd_ref[...])
    if final:
        y = y * lax.rsqrt(jnp.mean(y * y, axis=-1, keepdims=True) + EPS) * gf_ref[...]
    o_ref[0] = y


def _ffn(x_mix, m_lat, m_ctx, g, wu, wdw, bdw, wd, g_final, *, ctx_len, tm, fc, final):
    bsz, t, d = x_mix.shape
    d_ff = wd.shape[0]
    per = tm // SUBLANES
    last_blk = t // SUBLANES - 1
    skip = ctx_len // tm if final else 0
    return pl.pallas_call(
        functools.partial(_ffn_kernel, ctx_len=ctx_len, t_len=t, tm=tm, d_ff=d_ff, fc=fc,
                          final=final, first_tile=skip),
        out_shape=jax.ShapeDtypeStruct((bsz, t - skip * tm, d), F32),
        grid=(bsz, t // tm - skip),
        in_specs=[pl.BlockSpec((1, tm, d), lambda b, j: (b, j + skip, 0)),
                  pl.BlockSpec((1, SUBLANES, d),
                               lambda b, j: (b, jnp.maximum((j + skip) * per - 1, 0), 0)),
                  pl.BlockSpec((1, SUBLANES, d),
                               lambda b, j: (b, jnp.minimum((j + skip + 1) * per, last_blk), 0)),
                  pl.BlockSpec((1, N_MOD, d), lambda b, j: (b, 0, 0)),
                  pl.BlockSpec((N_MOD, d), lambda b, j: (0, 0)),
                  pl.BlockSpec((1, d), lambda b, j: (0, 0)),
                  _resident(wu.shape),
                  pl.BlockSpec(wdw.shape, lambda b, j: (0, 0)),
                  pl.BlockSpec(bdw.shape, lambda b, j: (0, 0)),
                  _resident(wd.shape),
                  pl.BlockSpec((1, d), lambda b, j: (0, 0))],
        out_specs=pl.BlockSpec((1, tm, d), lambda b, j: (b, j, 0)),
        compiler_params=_cparams("parallel", "parallel"),
        name="conv_ffn",
    )(x_mix, x_mix, x_mix, m_lat, m_ctx, g, wu, wdw, bdw, wd, g_final)


def _rope_tables(ctx_len, seq):
    rows = seq // GRID_W
    row = jnp.repeat(jnp.arange(rows), GRID_W).astype(F32)
    col = jnp.tile(jnp.arange(GRID_W), rows).astype(F32)
    half = HEAD_DIM // 2
    inv = ROPE_THETA ** (-jnp.arange(0, half, 2, dtype=F32) / half)
    ang = jnp.concatenate([row[:, None] * inv, col[:, None] * inv], axis=-1)
    cos = jnp.concatenate([jnp.ones((ctx_len, half), F32), jnp.cos(ang)], axis=0)
    sin = jnp.concatenate([jnp.zeros((ctx_len, half), F32), jnp.sin(ang)], axis=0)
    reps = LANES // HEAD_DIM
    pairs = lambda a, b: jnp.stack([a, b], axis=-1).reshape(a.shape[0], HEAD_DIM)
    return jnp.tile(pairs(cos, cos), (1, reps)), jnp.tile(pairs(-sin, sin), (1, reps))


def _lower_bounds(raw):
    p = jax.nn.softmax(raw.astype(F32), axis=0)
    return jnp.clip(jnp.cumsum(p, axis=0) - p[0], 0.0, 1.0)


def _row_tile(ctx_len):
    tm = 256
    while ctx_len % tm:
        tm //= 2
    return tm


def _wide_tile(t, cap):
    return max(tm for tm in range(BLOCK, cap + 1, BLOCK) if t % tm == 0)


def _ffn_chunk(d_ff):
    for fc in (256, 128):
        if d_ff % fc == 0:
            return fc
    return d_ff


def kernel(x, c, ctx, c_ctx, w_mod, b_mod, norm_mix, norm_ffn, w_in, sink_a, qn_b, kn_b, lb_fwd,
           lb_bwd, gn_c, w_branch, w_out, w_up, w_dw, b_dw, w_down, norm_final):
    bsz, seq, d = x.shape
    ctx_len = ctx.shape[1]
    depth = w_mod.shape[0]
    assert ctx_len % BLOCK == 0 and seq % BLOCK == 0 and ctx_len >= BLOCK and seq >= 3 * BLOCK
    assert d % LANES == 0
    tm = _row_tile(ctx_len)
    tm_proj = _wide_tile(ctx_len + seq, 384)
    tm_merge = _wide_tile(ctx_len + seq, 768)
    chunk = 64
    n_sub = max(n for n in (1, 2, 4) if ctx_len % (n * chunk) == 0 and seq % (n * chunk) == 0)
    fc = _ffn_chunk(w_down.shape[1])

    n_rows = -(-(bsz + 1) // SUBLANES) * SUBLANES
    c_rows = jnp.concatenate([c, c_ctx[None], jnp.zeros((n_rows - bsz - 1, d), F32)], axis=0)
    mods = _modulation(c_rows, w_mod, b_mod)
    m_lat = mods[:, :bsz].reshape(depth, bsz, N_MOD, d)
    m_ctx = mods[:, bsz].reshape(depth, N_MOD, d)

    w_in_p = w_in.astype(BF16)
    wb_bf = w_branch.astype(BF16)
    wo_bf = w_out.astype(BF16)
    wu_bf = w_up.astype(BF16)
    wd_bf = w_down.astype(BF16)
    qn_p = jnp.tile(qn_b, (1, LANES // HEAD_DIM))
    kn_p = jnp.tile(kn_b, (1, LANES // HEAD_DIM))
    lbf = _lower_bounds(lb_fwd)
    lbb = _lower_bounds(lb_bwd)
    cos, sin = _rope_tables(ctx_len, seq)
    lane = np.arange(LANES)
    bd = jnp.asarray(np.tile(lane[:, None] // HEAD_DIM == lane[None, :] // HEAD_DIM, (3, 1)), BF16)
    dmats = jnp.asarray(_scan_constants(chunk), BF16)

    x_all = jnp.concatenate([ctx, x], axis=1)
    for l in range(depth):
        qa, qb, kv, hq, hk, hlf, hv, hog, gates = _proj(
            x_all, m_lat[l], m_ctx[l], norm_mix[l][None], w_in_p[l], cos, sin,
            qn_p[l][None], kn_p[l][None], lbf[l][None], lbb[l][None], bd, ctx_len=ctx_len,
            tm=tm_proj)
        o_fwd, o_bwd = _scan(dmats, hq, hk, hlf, hv, ctx_len=ctx_len, chunk=chunk, n_sub=n_sub)
        ya = _attn_a(sink_a[l], qa, kv, ctx_len=ctx_len)
        yb = _attn_b(qb, kv, ctx_len=ctx_len)
        x_mix = _merge(x_all, ya, yb, o_fwd, o_bwd, hog, gates, gn_c[l][None], wb_bf[l], wo_bf[l],
                       m_lat[l], m_ctx[l], ctx_len=ctx_len, tm=tm_merge)
        x_all = _ffn(x_mix, m_lat[l], m_ctx[l], norm_ffn[l][None], wu_bf[l], w_dw[l],
                     b_dw[l][None], wd_bf[l], norm_final[None], ctx_len=ctx_len, tm=tm, fc=fc,
                     final=l == depth - 1)
    return x_all
```

```python
import functools

import numpy as np
import jax
import jax.numpy as jnp
from jax import lax
from jax.experimental import pallas as pl
from jax.experimental.pallas import tpu as pltpu

HEAD_DIM = 64
A_HEADS = 8
A_KV_HEADS = 2
B_HEADS = 8
B_KV_HEADS = 2
C_HEADS = 4
C_DK = 128
C_DV = 128
N_BRANCH = 3
WINDOW = 128
BLOCK = 128
GRID_W = 64
ROPE_THETA = 10000.0
N_MOD = 6
EPS = 1e-6
LOG2_E = 1.4426950408889634
NEG_BIG = -1e30
Q_WIDTH = A_HEADS * HEAD_DIM
KV_WIDTH = A_KV_HEADS * HEAD_DIM
C_WIDTH = C_HEADS * C_DK
FEAT_WIDTH = 2 * Q_WIDTH + 4 * KV_WIDTH + 5 * C_WIDTH
LANES = 128
SUBLANES = 8
VMEM_LIMIT = 56 * 1024 * 1024

F32 = jnp.float32
BF16 = jnp.bfloat16


def _cparams(*sem):
    return pltpu.CompilerParams(dimension_semantics=sem, vmem_limit_bytes=VMEM_LIMIT)


def _resident_layer(stack, layer):
    idx = (layer,) + (0,) * (stack.ndim - 1)
    return pl.BlockSpec((None,) + stack.shape[1:], lambda *_: idx, pipeline_mode=pl.Buffered(1))


def _dot(a, b):
    return jnp.dot(a, b, preferred_element_type=F32)


def _dot_nt(a, b):
    return lax.dot_general(a, b, (((1,), (1,)), ((), ())), preferred_element_type=F32)


def _dot_tn(a, b):
    return lax.dot_general(a, b, (((0,), (0,)), ((), ())), preferred_element_type=F32)


def _split3(x):
    hi = x.astype(BF16)
    r1 = x - hi.astype(F32)
    mid = r1.astype(BF16)
    lo = (r1 - mid.astype(F32)).astype(BF16)
    return hi, mid, lo


def _dot_exact_rhs(m3_bf16, x):
    return _dot(m3_bf16, jnp.concatenate(_split3(x), axis=0))


def _dot_exact_lhs(x, m3_bf16):
    return _dot(jnp.concatenate(_split3(x), axis=1), m3_bf16)


def _silu(x):
    return x * (1.0 / (1.0 + jnp.exp(-x)))


def _sigmoid(x):
    return 1.0 / (1.0 + jnp.exp(-x))


def _adaln(x, g, shift, scale):
    y = x * lax.rsqrt(jnp.mean(x * x, axis=-1, keepdims=True) + EPS)
    return (y * g) * (1.0 + scale) + shift


def _mod_rows(mlat_ref, mctx_ref, is_ctx):
    return jnp.where(is_ctx, mctx_ref[...], mlat_ref[0])


def _mod_picker(mlat_ref, mctx_ref, r0, tm, ctx_len):
    if ctx_len % tm == 0:
        mod = _mod_rows(mlat_ref, mctx_ref, r0 < ctx_len)
        return lambda i: mod[i:i + 1]
    is_ctx = r0 + lax.broadcasted_iota(jnp.int32, (tm, 1), 0) < ctx_len
    return lambda i: jnp.where(is_ctx, mctx_ref[i:i + 1], mlat_ref[0, i:i + 1])


def _mod_kernel(c_ref, w_ref, b_ref, o_ref):
    sc = _silu(c_ref[...]).astype(BF16)
    o_ref[0] = _dot(sc, w_ref[0].astype(BF16)) + b_ref[0]


def _modulation(c_rows, w_mod, b_mod):
    depth, d, n = w_mod.shape
    rows = c_rows.shape[0]
    tn = d
    return pl.pallas_call(
        _mod_kernel,
        out_shape=jax.ShapeDtypeStruct((depth, rows, n), F32),
        grid=(depth, n // tn),
        in_specs=[pl.BlockSpec((rows, d), lambda l, j: (0, 0)),
                  pl.BlockSpec((1, d, tn), lambda l, j: (l, 0, j)),
                  pl.BlockSpec((1, 1, tn), lambda l, j: (l, 0, j))],
        out_specs=pl.BlockSpec((1, rows, tn), lambda l, j: (l, 0, j)),
        compiler_params=_cparams("parallel", "parallel"),
        name="modulation",
    )(c_rows, w_mod, b_mod.reshape(depth, 1, n))


def _rope_group(t, cos, sin_signed, even):
    partner = jnp.where(even, pltpu.roll(t, LANES - 1, axis=1), pltpu.roll(t, 1, axis=1))
    return t * cos + partner * sin_signed


def _head_rms(t, gain, bd):
    ss = _dot_exact_lhs(t * t, bd)
    return t * lax.rsqrt(ss * (1.0 / HEAD_DIM) + EPS) * gain


def _forget_features(z, lb):
    e = jnp.exp(-jnp.abs(z))
    inv = 1.0 / (1.0 + e)
    small = e * inv
    pos = z >= 0.0
    sig = jnp.where(pos, inv, small)
    sig_neg = jnp.where(pos, small, inv)
    return (1.0 - lb) * sig_neg, jnp.log(lb + (1.0 - lb) * sig) * LOG2_E


def _proj_kernel(x_ref, mlat_ref, mctx_ref, g_ref, w_ref, cos_ref, sin_ref, qn_ref, kn_ref,
                 lbf_ref, lbb_ref, bd_ref,
                 qa_ref, qb_ref, kv_ref, hq_ref, hk_ref, hlf_ref, hv_ref, hog_ref, gt_ref,
                 *, ctx_len, tm, d_model):
    pick = _mod_picker(mlat_ref, mctx_ref, pl.program_id(1) * tm, tm, ctx_len)
    h = _adaln(x_ref[0], g_ref[...], pick(0), pick(1)).astype(BF16)
    cos = cos_ref[...]
    sin = sin_ref[...]
    bd = bd_ref[...]
    lane = lax.broadcasted_iota(jnp.int32, (tm, LANES), 1)
    even = (lane & 1) == 0
    scale = HEAD_DIM ** -0.5 * LOG2_E
    groups = Q_WIDTH // LANES
    off_akv = Q_WIDTH
    off_bq = off_akv + 2 * KV_WIDTH
    off_bkv = off_bq + Q_WIDTH
    off_c = off_bkv + 2 * KV_WIDTH

    def cols(lo, width):
        return _dot(h, w_ref[:, lo:lo + width])

    def lanes(t, gidx):
        return t[:, gidx * LANES:(gidx + 1) * LANES]

    def gate(n):
        lo = FEAT_WIDTH + n * d_model
        gt_ref[0, :, n * d_model:(n + 1) * d_model] = _sigmoid(
            _dot(h, w_ref[:, lo:lo + d_model])).astype(BF16)

    kf, lff = _forget_features(cols(off_c + C_WIDTH, C_WIDTH), lbf_ref[...])
    hk_ref[0, :, 0:C_WIDTH] = kf
    hlf_ref[0, :, 0:C_WIDTH] = lff
    gate(0)
    kb, lfb = _forget_features(cols(off_c + 2 * C_WIDTH, C_WIDTH), lbb_ref[...])
    hk_ref[0, :, C_WIDTH:2 * C_WIDTH] = kb
    hlf_ref[0, :, C_WIDTH:2 * C_WIDTH] = lfb
    gate(1)
    bq = cols(off_bq, Q_WIDTH)
    qb_ref[0] = jnp.concatenate(
        [_rope_group(_head_rms(lanes(bq, gi), qn_ref[...], bd), cos, sin, even) * scale
         for gi in range(groups)], axis=1).astype(BF16)
    hv_ref[0] = cols(off_c + 3 * C_WIDTH, C_WIDTH)
    akv = cols(off_akv, 2 * KV_WIDTH)
    bkv = cols(off_bkv, 2 * KV_WIDTH)
    kv_ref[0] = jnp.concatenate(
        [_rope_group(lanes(akv, 0), cos, sin, even),
         _rope_group(_head_rms(lanes(bkv, 0), kn_ref[...], bd), cos, sin, even),
         lanes(akv, 1), lanes(bkv, 1)], axis=1).astype(BF16)
    gate(2)
    aq = cols(0, Q_WIDTH)
    qa_ref[0] = jnp.concatenate(
        [_rope_group(lanes(aq, gi), cos, sin, even) * scale for gi in range(groups)],
        axis=1).astype(BF16)
    hq_ref[0] = _silu(cols(off_c, C_WIDTH))
    hog_ref[0] = _silu(cols(off_c + 4 * C_WIDTH, C_WIDTH))


def _proj(x_all, m_lat, m_ctx, g, w, cos, sin, qn, kn, lbf, lbb, bd, *, layer, ctx_len, tm):
    bsz, t, d = x_all.shape
    row =lambda width: pl.BlockSpec((1, tm, width), lambda b, j: (b, j, 0))
    vec = lambda width: pl.BlockSpec((1, width), lambda b, j: (0, 0))
    outs = [(Q_WIDTH, BF16), (Q_WIDTH, BF16), (4 * KV_WIDTH, BF16), (C_WIDTH, F32),
            (2 * C_WIDTH, F32), (2 * C_WIDTH, F32), (C_WIDTH, F32), (C_WIDTH, F32),
            (N_BRANCH * d, BF16)]
    return pl.pallas_call(
        functools.partial(_proj_kernel, ctx_len=ctx_len, tm=tm, d_model=d),
        out_shape=[jax.ShapeDtypeStruct((bsz, t, wd), dt) for wd, dt in outs],
        grid=(bsz, t // tm),
        in_specs=[row(d),
                  pl.BlockSpec((1, N_MOD, d), lambda b, j: (b, 0, 0)),
                  pl.BlockSpec((N_MOD, d), lambda b, j: (0, 0)),
                  vec(d),
                  _resident_layer(w, layer),
                  pl.BlockSpec((tm, LANES), lambda b, j: (j, 0)),
                  pl.BlockSpec((tm, LANES), lambda b, j: (j, 0)),
                  vec(LANES), vec(LANES), vec(C_WIDTH), vec(C_WIDTH),
                  pl.BlockSpec((3 * LANES, LANES), lambda b, j: (0, 0))],
        out_specs=[row(wd) for wd, _ in outs],
        compiler_params=_cparams("parallel", "parallel"),
        name="proj",
    )(x_all, m_lat, m_ctx, g, w, cos, sin, qn, kn, lbf, lbb, bd)


def _scan_constants(chunk):
    idx = np.arange(chunk)
    mats = []
    for direction in (0, 1):
        tri = (idx[None, :] <= idx[:, None]) if direction == 0 else (idx[None, :] >= idx[:, None])
        tri = tri.astype(np.float32)
        blocks = [tri, 1.0 - tri]
        h = chunk // 2
        while h >= 1:
            start = (idx // (2 * h)) * (2 * h)
            ref = start + h - 1 if direction == 0 else start + h
            is_q = ((idx // h) % 2) != direction
            diff = tri - tri[ref]
            blocks.append(np.where(is_q[:, None], diff, -diff))
            h //= 2
        mats.append(np.tile(np.concatenate(blocks, axis=0), (1, 3)))
    return np.stack(mats)


def _scan_kernel(dmat_ref, qf_ref, kf_ref, lff_ref, vf_ref, qb_ref, kb_ref, lfb_ref, vb_ref,
                 of_ref, ob_ref, st_ref, *, chunk, n_sub):
    @pl.when(pl.program_id(1) == 0)
    def _():
        st_ref[...] = jnp.zeros_like(st_ref)

    n_lev = chunk.bit_length() - 1
    dirs = (0, 1)
    rows = lambda sub: slice(sub * chunk, (sub + 1) * chunk)
    cols = lambda hh: slice(hh * C_DK, (hh + 1) * C_DK)
    q_all = (qf_ref[0], qb_ref[0])
    k_all = (kf_ref[0], kb_ref[0])
    v_all = (vf_ref[0].astype(BF16), vb_ref[0].astype(BF16))
    lf = (lff_ref[0], lfb_ref[0])
    e_all = {(d, sub): jnp.exp2(_dot_exact_rhs(dmat_ref[d], lf[d][rows(sub)]))
             for sub in range(n_sub) for d in dirs}

    row = lax.broadcasted_iota(jnp.int32, (chunk, 1), 0)
    r2 = lax.broadcasted_iota(jnp.int32, (chunk, chunk), 0)
    c2 = lax.broadcasted_iota(jnp.int32, (chunk, chunk), 1)
    heads = [(d, hh) for hh in range(C_HEADS) for d in dirs]

    scores = {}
    for sub in range(n_sub):
        for d, hh in heads:
            sl = cols(hh)
            q, k = q_all[d][rows(sub), sl], k_all[d][rows(sub), sl]
            later = (r2 > c2) if d == 0 else (r2 < c2)
            sc = jnp.where(r2 == c2, _dot_nt(q.astype(BF16), k.astype(BF16)), 0.0)
            for lev in range(n_lev):
                shift = n_lev - 1 - lev
                is_q = ((row >> shift) & 1) != d
                m = (jnp.where(is_q, q, k)
                     * e_all[d, sub][(2 + lev) * chunk:(3 + lev) * chunk, sl]).astype(BF16)
                pair = (later & ((r2 >> (shift + 1)) == (c2 >> (shift + 1)))
                        & ((r2 >> shift) != (c2 >> shift)))
                sc = jnp.where(pair, _dot_nt(m, m), sc)
            scores[d, sub, hh] = sc.astype(BF16)

    inter = {}
    for d, hh in heads:
        sl = cols(hh)
        st = st_ref[d, hh]
        for sub in (range(n_sub) if d == 0 else reversed(range(n_sub))):
            e_a = e_all[d, sub][0:chunk, sl]
            e_b = e_all[d, sub][chunk:2 * chunk, sl]
            inter[d, sub, hh] = _dot_nt((q_all[d][rows(sub), sl] * e_a).astype(BF16),
                                        st.astype(BF16))
            kd = (k_all[d][rows(sub), sl] * e_b).astype(BF16)
            st = st * (e_a[0:1] * e_b[0:1]) + _dot_tn(v_all[d][rows(sub), sl], kd)
        st_ref[d, hh] = st

    for d, o_ref in ((0, of_ref), (1, ob_ref)):
        for sub in range(n_sub):
            o_ref[0, rows(sub)] = jnp.concatenate(
                [inter[d, sub, hh] + _dot(scores[d, sub, hh], v_all[d][rows(sub), cols(hh)])
                 for hh in range(C_HEADS)], axis=1)


def _scan(dmats, hq, hk, hlf, hv, *, ctx_len, chunk, n_sub):
    bsz, t, _ = hq.shape
    blk = chunk * n_sub
    nc = t // blk
    nctx = ctx_len // blk

    def back(c):
        return jnp.where(c < nctx, nctx - 1 - c, nc - 1 + nctx - c)

    fwd = lambda col: pl.BlockSpec((1, blk, C_WIDTH), lambda b, c: (b, c, col))
    bwd = lambda col: pl.BlockSpec((1, blk, C_WIDTH), lambda b, c: (b, back(c), col))
    return pl.pallas_call(
        functools.partial(_scan_kernel, chunk=chunk, n_sub=n_sub),
        out_shape=[jax.ShapeDtypeStruct((bsz, t, C_WIDTH), F32)] * 2,
        grid=(bsz, nc),
        in_specs=[pl.BlockSpec(dmats.shape, lambda b, c: (0, 0, 0)),
                  fwd(0), fwd(0), fwd(0), fwd(0), bwd(0), bwd(1), bwd(1), bwd(0)],
        out_specs=[fwd(0), bwd(0)],
        scratch_shapes=[pltpu.VMEM((2, C_HEADS, C_DV, C_DK), F32)],
        compiler_params=_cparams("parallel", "arbitrary"),
        name="hgrn_scan",
    )(dmats, hq, hk, hlf, hv, hq, hk, hlf, hv)


def _stack_heads(q, kvh, group):
    return jnp.concatenate(
        [q[:, (kvh * group + g) * HEAD_DIM:(kvh * group + g + 1) * HEAD_DIM] for g in range(group)],
        axis=0)


def _unstack_heads(accs, group):
    pieces = []
    for kvh, acc in enumerate(accs):
        for g in range(group):
            blk = acc[:, g * BLOCK:(g + 1) * BLOCK].T
            o = blk * (1.0 / pltpu.roll(blk, HEAD_DIM, axis=1))
            pieces.append(o[:, kvh * HEAD_DIM:(kvh + 1) * HEAD_DIM])
    return jnp.concatenate(pieces, axis=1)


def _attn_a_kernel(sink_ref, q_ref, kv_ref, o_ref, *, ctx_len, t_len):
    i = pl.program_id(1)
    n_ctx_blocks = ctx_len // BLOCK
    group = A_HEADS // A_KV_HEADS
    band = 3 * BLOCK
    q = q_ref[0]

    def sink_col(kvh):
        r = lax.broadcasted_iota(jnp.int32, (group * BLOCK, 1), 0)
        col = jnp.full((group * BLOCK, 1), sink_ref[kvh * group], F32)
        for g in range(1, group):
            col = jnp.where(r >= g * BLOCK, sink_ref[kvh * group + g], col)
        return col * LOG2_E

    def ctx_kv(kvh):
        kc = kv_ref[0, 0:ctx_len, kvh * HEAD_DIM:(kvh + 1) * HEAD_DIM]
        vc = kv_ref[0, 0:ctx_len, 2 * KV_WIDTH + kvh * HEAD_DIM:2 * KV_WIDTH + (kvh + 1) * HEAD_DIM]
        return kc, vc

    def finish(parts):
        o_ref[0] = jnp.concatenate(
            [o[g * BLOCK:(g + 1) * BLOCK] for o in parts for g in range(group)], axis=1).astype(BF16)

    @pl.when(i < n_ctx_blocks)
    def _():
        parts = []
        for kvh in range(A_KV_HEADS):
            q4 = _stack_heads(q, kvh, group)
            kc, vc = ctx_kv(kvh)
            sk = sink_col(kvh)
            s = _dot_nt(q4, kc)
            m = jnp.maximum(jnp.max(s, axis=-1, keepdims=True), sk)
            p = jnp.exp2(s - m)
            den = jnp.sum(p, axis=-1, keepdims=True) + jnp.exp2(sk - m)
            parts.append(_dot(p.astype(BF16), vc) * (1.0 / den))
        finish(parts)

    @pl.when(i >= n_ctx_blocks)
    def _():
        n = i - n_ctx_blocks
        start = pl.multiple_of(jnp.minimum(ctx_len + (n - 1) * BLOCK, t_len - band), BLOCK)
        rq = lax.broadcasted_iota(jnp.int32, (group * BLOCK, band), 0)
        ck = lax.broadcasted_iota(jnp.int32, (group * BLOCK, band), 1)
        qpos = n * BLOCK + (rq & (BLOCK - 1))
        kpos = start - ctx_len + ck
        valid = (jnp.abs(kpos - qpos) <= WINDOW) & (kpos >= 0)
        parts = []
        for kvh in range(A_KV_HEADS):
            q4 = _stack_heads(q, kvh, group)
            kc, vc = ctx_kv(kvh)
            kl = kv_ref[0, pl.ds(start, band), kvh * HEAD_DIM:(kvh + 1) * HEAD_DIM]
            vl = kv_ref[0, pl.ds(start, band),
                        2 * KV_WIDTH + kvh * HEAD_DIM:2 * KV_WIDTH + (kvh + 1) * HEAD_DIM]
            sk = sink_col(kvh)
            s_ctx = _dot_nt(q4, kc)
            s_loc = jnp.where(valid, _dot_nt(q4, kl), NEG_BIG)
            m = jnp.maximum(jnp.maximum(jnp.max(s_ctx, axis=-1, keepdims=True),
                                        jnp.max(s_loc, axis=-1, keepdims=True)), sk)
            p_ctx = jnp.exp2(s_ctx - m)
            p_loc = jnp.exp2(s_loc - m)
            den = (jnp.sum(p_ctx, axis=-1, keepdims=True) + jnp.sum(p_loc, axis=-1, keepdims=True)
                   + jnp.exp2(sk - m))
            o = _dot(p_ctx.astype(BF16), vc) + _dot(p_loc.astype(BF16), vl)
            parts.append(o * (1.0 / den))
        finish(parts)


def _attn_a(sink, qa, kv, *, ctx_len):
    bsz, t, _ = qa.shape
    return pl.pallas_call(
        functools.partial(_attn_a_kernel, ctx_len=ctx_len, t_len=t),
        out_shape=jax.ShapeDtypeStruct((bsz, t, Q_WIDTH), BF16),
        grid=(bsz, t // BLOCK),
        in_specs=[pl.BlockSpec(memory_space=pltpu.SMEM),
                  pl.BlockSpec((1, BLOCK, Q_WIDTH), lambda b, i: (b, i, 0)),
                  pl.BlockSpec((1, t, 4 * KV_WIDTH), lambda b, i: (b, 0, 0))],
        out_specs=pl.BlockSpec((1, BLOCK, Q_WIDTH), lambda b, i: (b, i, 0)),
        compiler_params=_cparams("parallel", "arbitrary"),
        name="attn_window",
    )(sink, qa, kv)


def _attn_b_kernel(q_ref, kv_ref, o_ref, vt_ref, *, ctx_len, t_len, kc):
    i = pl.program_id(1)
    group = B_HEADS // B_KV_HEADS
    row = lax.broadcasted_iota(jnp.int32, (LANES, 1), 0)

    @pl.when(i == 0)
    def _():
        vt_ref[...] = kv_ref[0, :, 3 * KV_WIDTH:4 * KV_WIDTH].astype(F32).T.astype(BF16)

    def run(n_keys):
        q = q_ref[0]
        heads = range(B_KV_HEADS)
        q4 = [_stack_heads(q, kvh, group) for kvh in heads]
        own = [(row >= kvh * HEAD_DIM) & (row < (kvh + 1) * HEAD_DIM) for kvh in heads]
        bounds = [(k0, min(k0 + kc, n_keys)) for k0 in range(0, n_keys, kc)]

        def scores(c, kvh):
            k0, k1 = bounds[c]
            k = kv_ref[0, k0:k1, KV_WIDTH + kvh * HEAD_DIM:KV_WIDTH + (kvh + 1) * HEAD_DIM]
            return _dot_nt(k, q4[kvh])

        m = [jnp.full((1, group * BLOCK), NEG_BIG, F32) for _ in heads]
        acc = [jnp.zeros((LANES, group * BLOCK), F32) for _ in heads]
        s = [scores(0, kvh) for kvh in heads]
        for c, (k0, k1) in enumerate(bounds):
            s_next = [scores(c + 1, kvh) for kvh in heads] if c + 1 < len(bounds) else None
            vt = vt_ref[:, k0:k1]
            for kvh in heads:
                m_new = jnp.maximum(m[kvh], jnp.max(s[kvh], axis=0, keepdims=True))
                p = jnp.exp2(s[kvh] - m_new).astype(BF16)
                vt_ext = jnp.where(own[kvh], vt, jnp.ones_like(vt))
                acc[kvh] = acc[kvh] * jnp.exp2(m[kvh] - m_new) + _dot(vt_ext, p)
                m[kvh] = m_new
            s = s_next
        o_ref[0] = _unstack_heads(acc, group).astype(BF16)

    @pl.when(i < ctx_len // BLOCK)
    def _():
        run(ctx_len)

    @pl.when(i >= ctx_len // BLOCK)
    def _():
        run(t_len)


def _attn_b(qb, kv, *, ctx_len):
    bsz, t, _ = qb.shape
    return pl.pallas_call(
        functools.partial(_attn_b_kernel, ctx_len=ctx_len, t_len=t, kc=9 * BLOCK),
        out_shape=jax.ShapeDtypeStruct((bsz, t, Q_WIDTH), BF16),
        grid=(bsz, t // BLOCK),
        in_specs=[pl.BlockSpec((1, BLOCK, Q_WIDTH), lambda b, i: (b, i, 0)),
                  pl.BlockSpec((1, t, 4 * KV_WIDTH), lambda b, i: (b, 0, 0))],
        out_specs=pl.BlockSpec((1, BLOCK, Q_WIDTH), lambda b, i: (b, i, 0)),
        scratch_shapes=[pltpu.VMEM((LANES, t), BF16)],
        compiler_params=_cparams("parallel", "arbitrary"),
        name="attn_global",
    )(qb, kv)


def _merge_kernel(x_ref, ya_ref, yb_ref, of_ref, ob_ref, og_ref, gt_ref, gn_ref, wb_ref, wo_ref,
                  mlat_ref, mctx_ref, o_ref, *, ctx_len, tm, d_model):
    pick = _mod_picker(mlat_ref, mctx_ref, pl.program_id(1) * tm, tm, ctx_len)
    oc = of_ref[0] + ob_ref[0]
    og = og_ref[0]
    gn = gn_ref[...]
    heads = []
    for hh in range(C_HEADS):
        sl = slice(hh * C_DV, (hh + 1) * C_DV)
        t = oc[:, sl]
        y = t * lax.rsqrt(jnp.mean(t * t, axis=-1, keepdims=True) + EPS) * gn
        heads.append(y * og[:, sl])
    yc = jnp.concatenate(heads, axis=1).astype(BF16)
    merged = None
    for n, y in enumerate((ya_ref[0], yb_ref[0], yc)):
        up = _dot(y, wb_ref[n])
        term = gt_ref[0, :, n * d_model:(n + 1) * d_model].astype(F32) * up
        merged = term if merged is None else merged + term
    out = _dot(merged.astype(BF16), wo_ref[...])
    o_ref[0] = x_ref[0] + pick(2) * out


def _merge(x_all, ya, yb, o_fwd, o_bwd, hog, gates, gn, wb, wo, m_lat, m_ctx, *, layer, ctx_len,
           tm):
    bsz, t, d = x_all.shape
    row = lambda width: pl.BlockSpec((1, tm, width), lambda b, j: (b, j, 0))
    return pl.pallas_call(
        functools.partial(_merge_kernel, ctx_len=ctx_len, tm=tm, d_model=d),
        out_shape=jax.ShapeDtypeStruct((bsz, t, d), F32),
        grid=(bsz, t // tm),
        in_specs=[row(d), row(Q_WIDTH), row(Q_WIDTH), row(C_WIDTH), row(C_WIDTH),
                  row(C_WIDTH), row(N_BRANCH * d),
                  pl.BlockSpec((1, C_DV), lambda b, j: (0, 0)),
                  _resident_layer(wb, layer), _resident_layer(wo, layer),
                  pl.BlockSpec((1, N_MOD, d), lambda b, j: (b, 0, 0)),
                  pl.BlockSpec((N_MOD, d), lambda b, j: (0, 0))],
        out_specs=row(d),
        compiler_params=_cparams("parallel", "parallel"),
        name="merge",
    )(x_all, ya, yb, o_fwd, o_bwd, hog, gates, gn, wb, wo, m_lat, m_ctx)


def _ffn_kernel(x_ref, xp_ref, xn_ref, mlat_ref, mctx_ref, g_ref, wu_ref, wdw_ref, bdw_ref, wd_ref,
                gf_ref, o_ref, *, ctx_len, t_len, tm, d_ff, fc, final, first_tile):
    r0 = (pl.program_id(1) + first_tile) * tm
    mod = _mod_rows(mlat_ref, mctx_ref, r0 < ctx_len)
    x = x_ref[0]
    norm = lambda rows_: _adaln(rows_, g_ref[...], mod[3:4], mod[4:5])
    keep_prev = jnp.where((r0 == 0) | (r0 == ctx_len), 0.0, 1.0)
    keep_next = jnp.where((r0 + tm == ctx_len) | (r0 + tm == t_len), 0.0, 1.0)
    h = jnp.concatenate([norm(xp_ref[0]) * keep_prev, norm(x), norm(xn_ref[0]) * keep_next],
                        axis=0).astype(BF16)
    rows = tm + 2 * SUBLANES
    mid = slice(SUBLANES, SUBLANES + tm)

    def up(ci):
        return (_dot(h, wu_ref[:, ci * fc:(ci + 1) * fc]),
                _dot(h, wu_ref[:, d_ff + ci * fc:d_ff + (ci + 1) * fc]))

    def conv(u, lo):
        w = wdw_ref[:, lo:lo + fc]
        return (w[0:1] * pltpu.roll(u, 1, axis=0)[mid] + w[1:2] * u[mid]
                + w[2:3] * pltpu.roll(u, rows - 1, axis=0)[mid] + bdw_ref[:, lo:lo + fc])

    n_chunks = d_ff // fc
    ups = [up(0)]
    acts = []
    for ci in range(n_chunks):
        if ci + 1 < n_chunks:
            ups.append(up(ci + 1))
        ua, ug = ups[ci]
        acts.append((conv(ua, ci * fc) * _silu(conv(ug, d_ff + ci * fc))).astype(BF16))
    y = x + mod[5:6] * _dot(jnp.concatenate(acts, axis=1), wd_ref[...])
    if final:
        y = y * lax.rsqrt(jnp.mean(y * y, axis=-1, keepdims=True) + EPS) * gf_ref[...]
    o_ref[0] = y


def _ffn(x_mix, m_lat, m_ctx, g, wu, wdw, bdw, wd, g_final, *, layer, ctx_len, tm, fc, final):
    bsz, t, d = x_mix.shape
    d_ff = wd.shape[1]
    per = tm // SUBLANES
    last_blk = t // SUBLANES - 1
    skip = ctx_len // tm if final else 0
    return pl.pallas_call(
        functools.partial(_ffn_kernel, ctx_len=ctx_len, t_len=t, tm=tm, d_ff=d_ff, fc=fc,
                          final=final, first_tile=skip),
        out_shape=jax.ShapeDtypeStruct((bsz, t - skip * tm, d), F32),
        grid=(bsz, t // tm - skip),
        in_specs=[pl.BlockSpec((1, tm, d), lambda b, j: (b, j + skip, 0)),
                  pl.BlockSpec((1, SUBLANES, d),
                               lambda b, j: (b, jnp.maximum((j + skip) * per - 1, 0), 0)),
                  pl.BlockSpec((1, SUBLANES, d),
                               lambda b, j: (b, jnp.minimum((j + skip + 1) * per, last_blk), 0)),
                  pl.BlockSpec((1, N_MOD, d), lambda b, j: (b, 0, 0)),
                  pl.BlockSpec((N_MOD, d), lambda b, j: (0, 0)),
                  pl.BlockSpec((1, d), lambda b, j: (0, 0)),
                  _resident_layer(wu, layer),
                  pl.BlockSpec(wdw.shape, lambda b, j: (0, 0)),
                  pl.BlockSpec(bdw.shape, lambda b, j: (0, 0)),
                  _resident_layer(wd, layer),
                  pl.BlockSpec((1, d), lambda b, j: (0, 0))],
        out_specs=pl.BlockSpec((1, tm, d), lambda b, j: (b, j, 0)),
        compiler_params=_cparams("parallel", "parallel"),
        name="conv_ffn",
    )(x_mix, x_mix, x_mix, m_lat, m_ctx, g, wu, wdw, bdw, wd, g_final)


def _rope_tables(ctx_len, seq):
    rows = seq // GRID_W
    row = jnp.repeat(jnp.arange(rows), GRID_W).astype(F32)
    col = jnp.tile(jnp.arange(GRID_W), rows).astype(F32)
    half = HEAD_DIM // 2
    inv = ROPE_THETA ** (-jnp.arange(0, half, 2, dtype=F32) / half)
    ang = jnp.concatenate([row[:, None] * inv, col[:, None] * inv], axis=-1)
    cos = jnp.concatenate([jnp.ones((ctx_len, half), F32), jnp.cos(ang)], axis=0)
    sin = jnp.concatenate([jnp.zeros((ctx_len, half), F32), jnp.sin(ang)], axis=0)
    reps = LANES // HEAD_DIM
    pairs = lambda a, b: jnp.stack([a, b], axis=-1).reshape(a.shape[0], HEAD_DIM)
    return jnp.tile(pairs(cos, cos), (1, reps)), jnp.tile(pairs(-sin, sin), (1, reps))


def _lower_bounds(raw):
    p = jax.nn.softmax(raw.astype(F32), axis=0)
    return jnp.clip(jnp.cumsum(p, axis=0) - p[0], 0.0, 1.0)


def _row_tile(ctx_len):
    tm = 256
    while ctx_len % tm:
        tm //= 2
    return tm


def _wide_tile(t, cap):
    return max(tm for tm in range(BLOCK, cap + 1, BLOCK) if t % tm == 0)


def _ffn_chunk(d_ff):
    for fc in (256, 128):
        if d_ff % fc == 0:
            return fc
    return d_ff


def kernel(x, c, ctx, c_ctx, w_mod, b_mod, norm_mix, norm_ffn, w_in, sink_a, qn_b, kn_b, lb_fwd,
           lb_bwd, gn_c, w_branch, w_out, w_up, w_dw, b_dw, w_down, norm_final):
    bsz, seq, d = x.shape
    ctx_len = ctx.shape[1]
    depth = w_mod.shape[0]
    assert ctx_len % BLOCK == 0 and seq % BLOCK == 0 and ctx_len >= BLOCK and seq >= 3 * BLOCK
    assert d % LANES == 0
    tm = _row_tile(ctx_len)
    tm_proj = _wide_tile(ctx_len + seq, 384)
    tm_merge = _wide_tile(ctx_len + seq, 768)
    chunk = 64
    n_sub = max(n for n in (1, 2, 4) if ctx_len % (n * chunk) == 0 and seq % (n * chunk) == 0)
    fc = _ffn_chunk(w_down.shape[1])

    n_rows = -(-(bsz + 1) // SUBLANES) * SUBLANES
    c_rows = jnp.concatenate([c, c_ctx[None], jnp.zeros((n_rows - bsz - 1, d), F32)], axis=0)
    mods = _modulation(c_rows, w_mod, b_mod)
    m_lat = mods[:, :bsz].reshape(depth, bsz, N_MOD, d)
    m_ctx = mods[:, bsz].reshape(depth, N_MOD, d)

    w_in_p = w_in.astype(BF16)
    wb_bf = w_branch.astype(BF16)
    wo_bf = w_out.astype(BF16)
    wu_bf = w_up.astype(BF16)
    wd_bf = w_down.astype(BF16)
    qn_p = jnp.tile(qn_b, (1, LANES // HEAD_DIM))
    kn_p = jnp.tile(kn_b, (1, LANES // HEAD_DIM))
    lbf = _lower_bounds(lb_fwd)
    lbb = _lower_bounds(lb_bwd)
    cos, sin = _rope_tables(ctx_len, seq)
    lane = np.arange(LANES)
    bd = jnp.asarray(np.tile(lane[:, None] // HEAD_DIM == lane[None, :] // HEAD_DIM, (3, 1)), BF16)
    dmats = jnp.asarray(_scan_constants(chunk), BF16)

    x_all = jnp.concatenate([ctx, x], axis=1)
    for l in range(depth):
        qa, qb, kv, hq, hk, hlf, hv, hog, gates = _proj(
            x_all, m_lat[l], m_ctx[l], norm_mix[l][None], w_in_p, cos, sin,
            qn_p[l][None], kn_p[l][None], lbf[l][None], lbb[l][None], bd, layer=l,
            ctx_len=ctx_len, tm=tm_proj)
        o_fwd, o_bwd = _scan(dmats, hq, hk, hlf, hv, ctx_len=ctx_len, chunk=chunk, n_sub=n_sub)
        ya = _attn_a(sink_a[l], qa, kv, ctx_len=ctx_len)
        yb = _attn_b(qb, kv, ctx_len=ctx_len)
        x_mix = _merge(x_all, ya, yb, o_fwd, o_bwd, hog, gates, gn_c[l][None], wb_bf, wo_bf,
                       m_lat[l], m_ctx[l], layer=l, ctx_len=ctx_len, tm=tm_merge)
        x_all = _ffn(x_mix, m_lat[l], m_ctx[l], norm_ffn[l][None], wu_bf, w_dw[l],
                     b_dw[l][None], wd_bf, norm_final[None], layer=l, ctx_len=ctx_len, tm=tm,
                     fc=fc, final=l == depth - 1)
    return x_all
```
